```python
import jax, jax.numpy as jnp
from jax import lax
import numpy as np

D_MODEL = 1024
BATCH = 4
SEQ = 4096
DEPTH = 4

GRID_W = 64
CTX_LEN = 256
N_MIXERS = 4
Q_BLOCK = 128
ROPE_THETA = 10000.0
NORM_EPS = 1e-6
CONV_WIDTH = 31
CONV_PAD = CONV_WIDTH // 2
GQA_HEADS = 16
GQA_KV_HEADS = 4
GQA_GROUP = GQA_HEADS // GQA_KV_HEADS
GQA_HEAD_DIM = D_MODEL // GQA_HEADS
POOL_WINDOWS = (2, 4, 8, 16)
POOL_GROUP = D_MODEL // len(POOL_WINDOWS)
MLA_HEADS = 16
MLA_NOPE = 64
MLA_ROPE = 32
MLA_V = 64
MLA_QK = MLA_NOPE + MLA_ROPE
MLA_Q_LORA = 768
MLA_KV_LORA = 256
D_FF = 4 * D_MODEL
N_CONV = (DEPTH + 3) // N_MIXERS
N_GQA = (DEPTH + 2) // N_MIXERS
N_POOL = (DEPTH + 1) // N_MIXERS
N_MLA = DEPTH // N_MIXERS

kernel_name = 'hybrid_interleaved_dit_ctx_prefix'


def rms_norm(x, g):
    xf = x.astype(jnp.float32)
    y = xf * lax.rsqrt(jnp.mean(xf * xf, axis=-1, keepdims=True) + NORM_EPS)
    return (y * g.astype(jnp.float32)).astype(x.dtype)


def layer_norm(x, g, b):
    xf = x.astype(jnp.float32)
    mu = jnp.mean(xf, axis=-1, keepdims=True)
    var = jnp.mean(jnp.square(xf - mu), axis=-1, keepdims=True)
    y = (xf - mu) * lax.rsqrt(var + NORM_EPS)
    return (y * g.astype(jnp.float32) + b.astype(jnp.float32)).astype(x.dtype)


def modulate(h, shift, scale):
    return h * (1 + scale[..., None, :]) + shift[..., None, :]


def axial_tables(row, col, rot_dim):
    quarter = rot_dim // 4
    inv = ROPE_THETA ** (-jnp.arange(quarter, dtype=jnp.float32) / quarter)
    ang_r = row[:, None] * inv
    ang_c = col[:, None] * inv
    return (jnp.cos(ang_r), jnp.sin(ang_r), jnp.cos(ang_c), jnp.sin(ang_c))


def _rotate(v, cos, sin):
    a, b = jnp.split(v, 2, axis=-1)
    cos = cos.astype(v.dtype)
    sin = sin.astype(v.dtype)
    return jnp.concatenate([a * cos - b * sin, b * cos + a * sin], axis=-1)


def apply_axial_rope(x, tables):
    cr, sr, cc, sc = tables
    xr, xc = jnp.split(x, 2, axis=-1)
    return jnp.concatenate([_rotate(xr, cr, sr), _rotate(xc, cc, sc)], axis=-1)


def softmax_attend(q, k, v):
    s = jnp.einsum('bkgqd,bkld->bkgql', q, k).astype(jnp.float32) * (q.shape[-1] ** -0.5)
    p = jax.nn.softmax(s, axis=-1).astype(v.dtype)
    return jnp.einsum('bkgql,bkld->bkgqd', p, v)


def blocked_attend(q, k, v):
    b, kh, g, t, dq = q.shape
    nb = t // Q_BLOCK
    qb = q.reshape(b, kh, g, nb, Q_BLOCK, dq).transpose(3, 0, 1, 2, 4, 5)
    out = lax.map(lambda qq: softmax_attend(qq, k, v), qb)
    return out.transpose(1, 2, 3, 0, 4, 5).reshape(b, kh, g, t, v.shape[-1])


def merge_heads(o):
    b, kh, g, t, dv = o.shape
    return o.transpose(0, 3, 1, 2, 4).reshape(b, t, kh * g * dv)


def conv_module(h, w_pw1, w_dw, b_dw, ln_g, ln_b, w_pw2):
    u = h @ w_pw1
    a, gate = jnp.split(u, 2, axis=-1)
    u = a * jax.nn.sigmoid(gate)
    u = lax.conv_general_dilated(
        u, w_dw[:, None, :].astype(u.dtype), window_strides=(1,),
        padding=[(CONV_PAD, CONV_PAD)], dimension_numbers=('NWC', 'WIO', 'NWC'),
        feature_group_count=D_MODEL) + b_dw
    u = jax.nn.silu(layer_norm(u, ln_g, ln_b))
    return u @ w_pw2


def gqa_project(h, w_qkv, q_norm, k_norm):
    b, t, _ = h.shape
    nq = GQA_HEADS * GQA_HEAD_DIM
    nk = GQA_KV_HEADS * GQA_HEAD_DIM
    qkv = h @ w_qkv
    q = qkv[..., :nq].reshape(b, t, GQA_KV_HEADS, GQA_GROUP, GQA_HEAD_DIM).transpose(0, 2, 3, 1, 4)
    k = qkv[..., nq:nq + nk].reshape(b, t, GQA_KV_HEADS, GQA_HEAD_DIM).transpose(0, 2, 1, 3)
    v = qkv[..., nq + nk:].reshape(b, t, GQA_KV_HEADS, GQA_HEAD_DIM).transpose(0, 2, 1, 3)
    return rms_norm(q, q_norm), rms_norm(k, k_norm), v


def gqa_mixer(hl, hc, w_qkv, q_norm, k_norm, w_o, tables, with_ctx):
    ql, kl, vl = gqa_project(hl, w_qkv, q_norm, k_norm)
    qc, kc, vc = gqa_project(hc, w_qkv, q_norm, k_norm)
    ql = apply_axial_rope(ql, tables)
    kl = apply_axial_rope(kl, tables)
    k_all = jnp.concatenate([kc, kl], axis=2)
    v_all = jnp.concatenate([vc, vl], axis=2)
    ol = merge_heads(blocked_attend(ql, k_all, v_all)) @ w_o
    oc = merge_heads(softmax_attend(qc, kc, vc)) @ w_o if with_ctx else None
    return ol, oc


def pool_mixer(h, w_pool, scale):
    b, t, _ = h.shape
    hf = h.astype(jnp.float32)
    cs = jnp.concatenate([jnp.zeros((b, 1, D_MODEL), jnp.float32), jnp.cumsum(hf, axis=1)], axis=1)
    pos = jnp.arange(t)
    outs = []
    for g, w in enumerate(POOL_WINDOWS):
        lo = jnp.clip(pos - w // 2, 0, t)
        hi = jnp.clip(pos + w - w // 2, 0, t)
        csg = cs[..., g * POOL_GROUP:(g + 1) * POOL_GROUP]
        win = jnp.take(csg, hi, axis=1) - jnp.take(csg, lo, axis=1)
        cnt = (hi - lo).astype(jnp.float32)[:, None]
        outs.append(win / cnt - hf[..., g * POOL_GROUP:(g + 1) * POOL_GROUP])
    p = jnp.stack(outs, axis=2).astype(h.dtype)
    y = jnp.einsum('btgc,gcd->btgd', p, w_pool).reshape(b, t, D_MODEL)
    return y * scale


def mla_project(h, w_dq, q_lora_norm, w_uq, w_dkv, kv_lora_norm, w_ukv, q_norm, k_norm):
    b, t, _ = h.shape
    cq = rms_norm(h @ w_dq, q_lora_norm)
    q = (cq @ w_uq).reshape(b, t, MLA_HEADS, MLA_QK).transpose(0, 2, 1, 3)
    dkv = h @ w_dkv
    ckv = rms_norm(dkv[..., :MLA_KV_LORA], kv_lora_norm)
    k_rope = dkv[..., MLA_KV_LORA:]
    kv = (ckv @ w_ukv).reshape(b, t, MLA_HEADS, MLA_NOPE + MLA_V).transpose(0, 2, 1, 3)
    k_nope, v = kv[..., :MLA_NOPE], kv[..., MLA_NOPE:]
    k = jnp.concatenate([k_nope, jnp.broadcast_to(k_rope[:, None], (b, MLA_HEADS, t, MLA_ROPE))], axis=-1)
    return rms_norm(q, q_norm), rms_norm(k, k_norm), v


def rope_tail(x, tables):
    return jnp.concatenate([x[..., :MLA_NOPE], apply_axial_rope(x[..., MLA_NOPE:], tables)], axis=-1)


def mla_mixer(hl, hc, w_dq, q_lora_norm, w_uq, w_dkv, kv_lora_norm, w_ukv, q_norm, k_norm, w_o,
              tables, with_ctx):
    args = (w_dq, q_lora_norm, w_uq, w_dkv, kv_lora_norm, w_ukv, q_norm, k_norm)
    ql, kl, vl = mla_project(hl, *args)
    qc, kc, vc = mla_project(hc, *args)
    ql = rope_tail(ql, tables)[:, :, None]
    kl = rope_tail(kl, tables)
    k_all = jnp.concatenate([kc, kl], axis=2)
    v_all = jnp.concatenate([vc, vl], axis=2)
    ol = merge_heads(blocked_attend(ql, k_all, v_all)) @ w_o
    oc = merge_heads(softmax_attend(qc[:, :, None], kc, vc)) @ w_o if with_ctx else None
    return ol, oc


def sq_relu_mlp(h, w1, w2):
    return jnp.square(jax.nn.relu(h @ w1)) @ w2


def setup_inputs(seed: int = 0) -> dict:
    key = jax.random.key(seed)
    ks = jax.random.split(key, 31)
    f32 = jnp.float32

    def nrm(k, shape, scale):
        return jax.random.normal(k, shape, f32) * scale

    def gain(k, shape):
        return 1.0 + 0.05 * jax.random.normal(k, shape, f32)

    d = D_MODEL
    return {
        'x': nrm(ks[0], (BATCH, SEQ, d), 1.0),
        'c': nrm(ks[1], (BATCH, d), 1.0),
        'ctx': nrm(ks[2], (BATCH, CTX_LEN, d), 1.0),
        'c_ctx': nrm(ks[3], (d,), 1.0),
        'norm1_g': gain(ks[4], (DEPTH, d)),
        'norm2_g': gain(ks[5], (DEPTH, d)),
        'w_mod': nrm(ks[6], (DEPTH, d, 6 * d), 0.5 * d ** -0.5),
        'b_mod': nrm(ks[7], (DEPTH, 6 * d), 0.02),
        'w_ff1': nrm(ks[8], (DEPTH, d, D_FF), d ** -0.5),
        'w_ff2': nrm(ks[9], (DEPTH, D_FF, d), D_FF ** -0.5),
        'conv_w_pw1': nrm(ks[10], (N_CONV, d, 2 * d), d ** -0.5),
        'conv_w_dw': nrm(ks[11], (N_CONV, CONV_WIDTH, d), CONV_WIDTH ** -0.5),
        'conv_b_dw': nrm(ks[12], (N_CONV, d), 0.02),
        'conv_ln_g': gain(ks[13], (N_CONV, d)),
        'conv_ln_b': nrm(ks[14], (N_CONV, d), 0.02),
        'conv_w_pw2': nrm(ks[15], (N_CONV, d, d), d ** -0.5),
        'gqa_w_qkv': nrm(ks[16], (N_GQA, d, (GQA_HEADS + 2 * GQA_KV_HEADS) * GQA_HEAD_DIM), d ** -0.5),
        'gqa_q_norm': gain(ks[17], (N_GQA, GQA_HEAD_DIM)),
        'gqa_k_norm': gain(ks[18], (N_GQA, GQA_HEAD_DIM)),
        'gqa_w_o': nrm(ks[19], (N_GQA, GQA_HEADS * GQA_HEAD_DIM, d), (GQA_HEADS * GQA_HEAD_DIM) ** -0.5),
        'pool_w': nrm(ks[20], (N_POOL, len(POOL_WINDOWS), POOL_GROUP, POOL_GROUP), POOL_GROUP ** -0.5),
        'pool_scale': gain(ks[21], (N_POOL, d)),
        'mla_w_dq': nrm(ks[22], (N_MLA, d, MLA_Q_LORA), d ** -0.5),
        'mla_q_lora_norm': gain(ks[23], (N_MLA, MLA_Q_LORA)),
        'mla_w_uq': nrm(ks[24], (N_MLA, MLA_Q_LORA, MLA_HEADS * MLA_QK), MLA_Q_LORA ** -0.5),
        'mla_w_dkv': nrm(ks[25], (N_MLA, d, MLA_KV_LORA + MLA_ROPE), d ** -0.5),
        'mla_kv_lora_norm': gain(ks[26], (N_MLA, MLA_KV_LORA)),
        'mla_w_ukv': nrm(ks[27], (N_MLA, MLA_KV_LORA, MLA_HEADS * (MLA_NOPE + MLA_V)), MLA_KV_LORA ** -0.5),
        'mla_q_norm': gain(ks[28], (N_MLA, MLA_QK)),
        'mla_k_norm': gain(ks[29], (N_MLA, MLA_QK)),
        'mla_w_o': nrm(ks[30], (N_MLA, MLA_HEADS * MLA_V, d), (MLA_HEADS * MLA_V) ** -0.5),
    }


def reference(x, c, ctx, c_ctx, norm1_g, norm2_g, w_mod, b_mod, w_ff1, w_ff2,
              conv_w_pw1, conv_w_dw, conv_b_dw, conv_ln_g, conv_ln_b, conv_w_pw2,
              gqa_w_qkv, gqa_q_norm, gqa_k_norm, gqa_w_o, pool_w, pool_scale,
              mla_w_dq, mla_q_lora_norm, mla_w_uq, mla_w_dkv, mla_kv_lora_norm, mla_w_ukv,
              mla_q_norm, mla_k_norm, mla_w_o):
    seq = x.shape[1]
    rows = seq // GRID_W
    row_ids = jnp.repeat(jnp.arange(rows, dtype=jnp.float32), GRID_W)
    col_ids = jnp.tile(jnp.arange(GRID_W, dtype=jnp.float32), rows)
    tab_gqa = axial_tables(row_ids, col_ids, GQA_HEAD_DIM)
    tab_mla = axial_tables(row_ids, col_ids, MLA_ROPE)

    sc = jax.nn.silu(c)
    scc = jax.nn.silu(c_ctx)
    xl, xc = x, ctx
    for i in range(DEPTH):
        m = i % N_MIXERS
        j = i // N_MIXERS
        with_ctx = i < DEPTH - 1
        mod_l = jnp.split(sc @ w_mod[i] + b_mod[i], 6, axis=-1)
        mod_c = jnp.split(scc @ w_mod[i] + b_mod[i], 6, axis=-1)

        hl = modulate(rms_norm(xl, norm1_g[i]), mod_l[0], mod_l[1])
        hc = modulate(rms_norm(xc, norm1_g[i]), mod_c[0], mod_c[1])
        if m == 0:
            cargs = (conv_w_pw1[j], conv_w_dw[j], conv_b_dw[j], conv_ln_g[j], conv_ln_b[j], conv_w_pw2[j])
            ol = conv_module(hl, *cargs)
            oc = conv_module(hc, *cargs) if with_ctx else None
        elif m == 1:
            ol, oc = gqa_mixer(hl, hc, gqa_w_qkv[j], gqa_q_norm[j], gqa_k_norm[j], gqa_w_o[j],
                               tab_gqa, with_ctx)
        elif m == 2:
            ol = pool_mixer(hl, pool_w[j], pool_scale[j])
            oc = pool_mixer(hc, pool_w[j], pool_scale[j]) if with_ctx else None
        else:
            ol, oc = mla_mixer(hl, hc, mla_w_dq[j], mla_q_lora_norm[j], mla_w_uq[j], mla_w_dkv[j],
                               mla_kv_lora_norm[j], mla_w_ukv[j], mla_q_norm[j], mla_k_norm[j],
                               mla_w_o[j], tab_mla, with_ctx)
        xl = xl + mod_l[2][:, None, :] * ol
        if with_ctx:
            xc = xc + mod_c[2] * oc

        hl = modulate(rms_norm(xl, norm2_g[i]), mod_l[3], mod_l[4])
        xl = xl + mod_l[5][:, None, :] * sq_relu_mlp(hl, w_ff1[i], w_ff2[i])
        if with_ctx:
            hc = modulate(rms_norm(xc, norm2_g[i]), mod_c[3], mod_c[4])
            xc = xc + mod_c[5] * sq_relu_mlp(hc, w_ff1[i], w_ff2[i])
    return xl
```

```python
import functools

import jax
import jax.numpy as jnp
from jax import lax
from jax.experimental import pallas as pl
from jax.experimental.pallas import tpu as pltpu

D_MODEL = 1024
GRID_W = 64
ROPE_THETA = 10000.0
NORM_EPS = 1e-6
CONV_WIDTH = 31
CONV_PAD = CONV_WIDTH // 2
GQA_HEADS = 16
GQA_KV_HEADS = 4
GQA_HEAD_DIM = 64
POOL_WINDOWS = (2, 4, 8, 16)
POOL_GROUP = 256
MLA_HEADS = 16
MLA_NOPE = 64
MLA_ROPE = 32
MLA_V = 64
MLA_QK = MLA_NOPE + MLA_ROPE
MLA_Q_LORA = 768
MLA_KV_LORA = 256
D_FF = 4 * D_MODEL
N_MIXERS = 4

HALO = 16
HEADS_PER_STEP = 4
V_DIM = 64
VMEM_LIMIT = 52 * 1024 * 1024
BF = jnp.bfloat16
F32 = jnp.float32


def _cparams(n_axes):
    return pltpu.CompilerParams(dimension_semantics=("arbitrary",) * n_axes, vmem_limit_bytes=VMEM_LIMIT)


def _resident(shape):
    nd = len(shape)
    return pl.BlockSpec(shape, lambda *_: (0,) * nd, pipeline_mode=pl.Buffered(1))


def _norm_mod(x, g, shift, scale):
    ms = jnp.mean(x * x, axis=-1, keepdims=True)
    y = x * lax.rsqrt(ms + NORM_EPS) * g
    return y * (1.0 + scale) + shift


def _rms_rows(xt, gain):
    ms = jnp.mean(xt * xt, axis=0, keepdims=True)
    return xt * lax.rsqrt(ms + NORM_EPS) * gain


def _rope_rows(xt, cos, sin, quarter):
    q = quarter
    rot = jnp.concatenate([xt[q:2 * q], xt[0:q], xt[3 * q:4 * q], xt[2 * q:3 * q]], axis=0)
    return xt * cos + rot * sin


def _mod_kernel(c_ref, w_ref, b_ref, o_ref):
    c = c_ref[...]
    s = (c * jax.nn.sigmoid(c)).astype(BF)
    o_ref[0] = jnp.dot(s, w_ref[0].astype(BF), preferred_element_type=F32) + b_ref[0]


def _mod_call(cvec, w_mod, b_mod):
    depth, d, n = w_mod.shape
    rows = cvec.shape[0]
    tn = 1536
    return pl.pallas_call(
        _mod_kernel,
        out_shape=jax.ShapeDtypeStruct((depth, rows, n), F32),
        grid=(depth, n // tn),
        in_specs=[pl.BlockSpec((rows, d), lambda l, j: (0, 0)),
                  pl.BlockSpec((1, d, tn), lambda l, j: (l, 0, j)),
                  pl.BlockSpec((1, 1, tn), lambda l, j: (l, 0, j))],
        out_specs=pl.BlockSpec((1, rows, tn), lambda l, j: (l, 0, j)),
        compiler_params=_cparams(2), name="adaln_mod",
    )(cvec, w_mod, b_mod.reshape(depth, 1, n))


def _post_kernel(has_oscale, x_ref, a_ref, mod_ref, g2_ref, wo_ref, w1_ref, w2_ref, *rest):
    if has_oscale:
        os_ref, o_ref = rest
    else:
        (o_ref,) = rest
    mod = mod_ref[0]
    y = jnp.dot(a_ref[0], wo_ref[...], preferred_element_type=F32)
    if has_oscale:
        y = y * os_ref[...]
    x1 = x_ref[0] + mod[2:3] * y
    h = _norm_mod(x1, g2_ref[...], mod[3:4], mod[4:5]).astype(BF)
    ffc = D_MODEL
    acc = jnp.zeros_like(x1)
    for c in range(D_FF // ffc):
        t = jnp.dot(h, w1_ref[:, c * ffc:(c + 1) * ffc], preferred_element_type=F32)
        t = jnp.maximum(t, 0.0)
        acc = acc + jnp.dot((t * t).astype(BF), w2_ref[c * ffc:(c + 1) * ffc, :], preferred_element_type=F32)
    o_ref[0] = x1 + mod[5:6] * acc


def _post_call(x, a, mod, g2, wo, w1, w2, oscale=None):
    b, t, d = x.shape
    tm = min(t, 512)
    tile = lambda: pl.BlockSpec((1, tm, d), lambda bi, i: (bi, i, 0))
    in_specs = [tile(), tile(), pl.BlockSpec((1, 8, d), lambda bi, i: (bi, 0, 0)),
                _resident((1, d)), _resident(wo.shape), _resident(w1.shape), _resident(w2.shape)]
    args = [x, a, mod, g2, wo, w1, w2]
    if oscale is not None:
        in_specs.append(_resident((1, d)))
        args.append(oscale)
    return pl.pallas_call(
        functools.partial(_post_kernel, oscale is not None),
        out_shape=jax.ShapeDtypeStruct(x.shape, F32),
        grid=(b, t // tm), in_specs=in_specs, out_specs=tile(),
        compiler_params=_cparams(2), name="outproj_mlp",
    )(*args)


def _glu_kernel(x_ref, mod_ref, g_ref, w_ref, o_ref):
    mod = mod_ref[0]
    h = _norm_mod(x_ref[0], g_ref[...], mod[0:1], mod[1:2]).astype(BF)
    u = jnp.dot(h, w_ref[...], preferred_element_type=F32)
    o_ref[0] = u[:, :D_MODEL] * jax.nn.sigmoid(u[:, D_MODEL:])


def _glu_call(x, mod, g, w):
    b, t, d = x.shape
    tm = min(t, 512)
    tile = lambda: pl.BlockSpec((1, tm, d), lambda bi, i: (bi, i, 0))
    return pl.pallas_call(
        _glu_kernel, out_shape=jax.ShapeDtypeStruct(x.shape, F32), grid=(b, t // tm),
        in_specs=[tile(), pl.BlockSpec((1, 8, d), lambda bi, i: (bi, 0, 0)), _resident((1, d)), _resident(w.shape)],
        out_specs=tile(), compiler_params=_cparams(2), name="conv_pw1_glu",
    )(x, mod, g, w)


def _halo_specs(t, tm, d):
    r = tm // HALO
    last = t // HALO - 1
    return [pl.BlockSpec((1, tm, d), lambda bi, i: (bi, i, 0)),
            pl.BlockSpec((1, HALO, d), lambda bi, i: (bi, jnp.maximum(i * r - 1, 0), 0)),
            pl.BlockSpec((1, HALO, d), lambda bi, i: (bi, jnp.minimum((i + 1) * r, last), 0))]


def _dwconv_kernel(tm, u_ref, up_ref, un_ref, w_ref, b_ref, lg_ref, lb_ref, o_ref, win_ref, y_ref):
    i = pl.program_id(1)
    n = pl.num_programs(1)
    win_ref[0:HALO] = jnp.where(i > 0, up_ref[0], 0.0)
    win_ref[HALO:HALO + tm] = u_ref[0]
    win_ref[HALO + tm:] = jnp.where(i < n - 1, un_ref[0], 0.0)
    rc, cc = 64, 256
    for c0 in range(0, D_MODEL, cc):
        for r0 in range(0, tm, rc):
            acc = jnp.broadcast_to(b_ref[:, c0:c0 + cc], (rc, cc))
            for j in range(CONV_WIDTH):
                acc = acc + win_ref[pl.ds(r0 + j + HALO - CONV_PAD, rc), c0:c0 + cc] * w_ref[pl.ds(j, 1), c0:c0 + cc]
            y_ref[r0:r0 + rc, c0:c0 + cc] = acc
    y = y_ref[...]
    mu = jnp.mean(y, axis=-1, keepdims=True)
    yc = y - mu
    var = jnp.mean(yc * yc, axis=-1, keepdims=True)
    z = yc * lax.rsqrt(var + NORM_EPS) * lg_ref[...] + lb_ref[...]
    o_ref[0] = (z * jax.nn.sigmoid(z)).astype(BF)


def _dwconv_call(u, w_dw, b_dw, ln_g, ln_b):
    b, t, d = u.shape
    tm = 128
    return pl.pallas_call(
        functools.partial(_dwconv_kernel, tm),
        out_shape=jax.ShapeDtypeStruct(u.shape, BF), grid=(b, t // tm),
        in_specs=_halo_specs(t, tm, d) + [_resident(w_dw.shape), _resident((1, d)), _resident((1, d)), _resident((1, d))],
        out_specs=pl.BlockSpec((1, tm, d), lambda bi, i: (bi, i, 0)),
        scratch_shapes=[pltpu.VMEM((tm + 2 * HALO, d), F32), pltpu.VMEM((tm, d), F32)],
        compiler_params=_cparams(2), name="dwconv_ln_silu",
    )(u, u, u, w_dw, b_dw, ln_g, ln_b)


def _pool_kernel(tm, t_total, x_ref, xp_ref, xn_ref, mod_ref, g_ref, o_ref, win_ref):
    i = pl.program_id(1)
    n = pl.num_programs(1)
    mod = mod_ref[0]
    nm = lambda v: _norm_mod(v, g_ref[...], mod[0:1], mod[1:2])
    win_ref[0:HALO] = jnp.where(i > 0, nm(xp_ref[0]), 0.0)
    win_ref[HALO:HALO + tm] = nm(x_ref[0])
    win_ref[HALO + tm:] = jnp.where(i < n - 1, nm(xn_ref[0]), 0.0)
    pos = i * tm + lax.broadcasted_iota(jnp.int32, (tm, 1), 0)
    for gi, w in enumerate(POOL_WINDOWS):
        c0 = gi * POOL_GROUP
        half = w // 2
        acc = win_ref[pl.ds(HALO - half, tm), c0:c0 + POOL_GROUP]
        for o in range(-half + 1, w - half):
            acc = acc + win_ref[pl.ds(HALO + o, tm), c0:c0 + POOL_GROUP]
        lo = jnp.maximum(pos - half, 0)
        hi = jnp.minimum(pos + (w - half), t_total)
        cnt = (hi - lo).astype(F32)
        o_ref[0, :, c0:c0 + POOL_GROUP] = (acc / cnt - win_ref[HALO:HALO + tm, c0:c0 + POOL_GROUP]).astype(BF)


def _pool_call(x, mod, g):
    b, t, d = x.shape
    tm = 256
    return pl.pallas_call(
        functools.partial(_pool_kernel, tm, t),
        out_shape=jax.ShapeDtypeStruct(x.shape, BF), grid=(b, t // tm),
        in_specs=_halo_specs(t, tm, d) + [pl.BlockSpec((1, 8, d), lambda bi, i: (bi, 0, 0)), _resident((1, d))],
        out_specs=pl.BlockSpec((1, tm, d), lambda bi, i: (bi, i, 0)),
        scratch_shapes=[pltpu.VMEM((tm + 2 * HALO, d), F32)],
        compiler_params=_cparams(2), name="pool_windows",
    )(x, x, x, mod, g)


def _gqa_proj_kernel(rope, x_ref, mod_ref, g_ref, wt_ref, qg_ref, kg_ref, *rest):
    if rope:
        cos_ref, sin_ref, q_ref, k_ref, v_ref, y_ref = rest
    else:
        q_ref, k_ref, v_ref, y_ref = rest
    mod = mod_ref[0]
    h = _norm_mod(x_ref[0], g_ref[...], mod[0:1], mod[1:2]).astype(BF)
    y_ref[...] = lax.dot_general(wt_ref[...], h, (((1,), (1,)), ((), ())), preferred_element_type=F32)
    hd = GQA_HEAD_DIM
    nq = GQA_HEADS * hd
    nk = GQA_KV_HEADS * hd

    def head(row0, gain_ref):
        z = _rms_rows(y_ref[row0:row0 + hd], gain_ref[...])
        if rope:
            z = _rope_rows(z, cos_ref[...], sin_ref[...], hd // 4)
        return z

    for hh in range(GQA_HEADS):
        q_ref[0, hh] = (head(hh * hd, qg_ref) * (hd ** -0.5)).astype(BF)
    for hh in range(GQA_KV_HEADS):
        k_ref[0, hh] = head(nq + hh * hd, kg_ref).T.astype(BF)
        v_ref[0, hh, 0] = y_ref[nq + nk + hh * hd:nq + nk + (hh + 1) * hd].astype(BF)


def _gqa_proj_call(x, mod, g, wt, qg, kg, tables):
    b, t, d = x.shape
    tm = min(t, 512)
    nt = t // tm
    hd = GQA_HEAD_DIM
    rope = tables is not None
    in_specs = [pl.BlockSpec((1, tm, d), lambda bi, i: (bi, i, 0)), pl.BlockSpec((1, 8, d), lambda bi, i: (bi, 0, 0)),
                _resident((1, d)), _resident(wt.shape), _resident((hd, 1)), _resident((hd, 1))]
    args = [x, mod, g, wt, qg, kg]
    if rope:
        in_specs += [pl.BlockSpec((hd, tm), lambda bi, i: (0, i))] * 2
        args += list(tables)
    return pl.pallas_call(
        functools.partial(_gqa_proj_kernel, rope),
        out_shape=(jax.ShapeDtypeStruct((b, GQA_HEADS, hd, t), BF),
                   jax.ShapeDtypeStruct((b, GQA_KV_HEADS, t, hd), BF),
                   jax.ShapeDtypeStruct((b, GQA_KV_HEADS, nt, hd, tm), BF)),
        grid=(b, nt), in_specs=in_specs,
        out_specs=(pl.BlockSpec((1, GQA_HEADS, hd, tm), lambda bi, i: (bi, 0, 0, i)),
                   pl.BlockSpec((1, GQA_KV_HEADS, tm, hd), lambda bi, i: (bi, 0, i, 0)),
                   pl.BlockSpec((1, GQA_KV_HEADS, 1, hd, tm), lambda bi, i: (bi, 0, i, 0, 0))),
        scratch_shapes=[pltpu.VMEM((wt.shape[0], tm), F32)],
        compiler_params=_cparams(2), name="gqa_qkv_proj",
    )(*args)


def _mla_proj_kernel(rope, need_q, x_ref, mod_ref, g_ref, wdq_ref, qln_ref, wuq_ref, wdkv_ref, kvln_ref, wukv_ref,
                     qg_ref, kg_ref, *rest):
    rest = list(rest)
    if rope:
        cos_ref, sin_ref = rest[:2]
        rest = rest[2:]
    if need_q:
        q_ref, k_ref, v_ref, yq_ref, ykv_ref = rest
    else:
        k_ref, v_ref, ykv_ref = rest
    mod = mod_ref[0]
    nt_dims = (((1,), (1,)), ((), ()))
    h = _norm_mod(x_ref[0], g_ref[...], mod[0:1], mod[1:2]).astype(BF)

    def tail_rope(z):
        if not rope:
            return z
        return jnp.concatenate([z[:MLA_NOPE], _rope_rows(z[MLA_NOPE:], cos_ref[...], sin_ref[...], MLA_ROPE // 4)], axis=0)

    if need_q:
        cq = lax.dot_general(wdq_ref[...], h, nt_dims, preferred_element_type=F32)
        cq = _rms_rows(cq, qln_ref[...]).astype(BF)
        yq_ref[...] = jnp.dot(wuq_ref[...], cq, preferred_element_type=F32)
        for hh in range(MLA_HEADS):
            z = tail_rope(_rms_rows(yq_ref[hh * MLA_QK:(hh + 1) * MLA_QK], qg_ref[...]))
            q_ref[0, hh] = (z * (MLA_QK ** -0.5)).astype(BF)
    dkv = lax.dot_general(wdkv_ref[...], h, nt_dims, preferred_element_type=F32)
    ckv = _rms_rows(dkv[:MLA_KV_LORA], kvln_ref[...]).astype(BF)
    k_rope = dkv[MLA_KV_LORA:]
    ykv_ref[...] = jnp.dot(wukv_ref[...], ckv, preferred_element_type=F32)
    per = MLA_NOPE + MLA_V
    for hh in range(MLA_HEADS):
        kf = jnp.concatenate([ykv_ref[hh * per:hh * per + MLA_NOPE], k_rope], axis=0)
        k_ref[0, hh] = tail_rope(_rms_rows(kf, kg_ref[...])).T.astype(BF)
        v_ref[0, hh, 0] = ykv_ref[hh * per + MLA_NOPE:(hh + 1) * per].astype(BF)


def _mla_proj_call(x, mod, g, wdq_t, qln, wuq_t, wdkv_t, kvln, wukv_t, qg, kg, tables, need_q):
    b, t, d = x.shape
    tm = min(t, 512)
    nt = t // tm
    rope = tables is not None
    in_specs = [pl.BlockSpec((1, tm, d), lambda bi, i: (bi, i, 0)), pl.BlockSpec((1, 8, d), lambda bi, i: (bi, 0, 0)),
                _resident((1, d)), _resident(wdq_t.shape), _resident(qln.shape), _resident(wuq_t.shape),
                _resident(wdkv_t.shape), _resident(kvln.shape), _resident(wukv_t.shape),
                _resident(qg.shape), _resident(kg.shape)]
    args = [x, mod, g, wdq_t, qln, wuq_t, wdkv_t, kvln, wukv_t, qg, kg]
    if rope:
        in_specs += [pl.BlockSpec((MLA_ROPE, tm), lambda bi, i: (0, i))] * 2
        args += list(tables)
    out_shape = [jax.ShapeDtypeStruct((b, MLA_HEADS, t, MLA_QK), BF),
                 jax.ShapeDtypeStruct((b, MLA_HEADS, nt, MLA_V, tm), BF)]
    out_specs = [pl.BlockSpec((1, MLA_HEADS, tm, MLA_QK), lambda bi, i: (bi, 0, i, 0)),
                 pl.BlockSpec((1, MLA_HEADS, 1, MLA_V, tm), lambda bi, i: (bi, 0, i, 0, 0))]
    scratch = [pltpu.VMEM((wukv_t.shape[0], tm), F32)]
    if need_q:
        out_shape.insert(0, jax.ShapeDtypeStruct((b, MLA_HEADS, MLA_QK, t), BF))
        out_specs.insert(0, pl.BlockSpec((1, MLA_HEADS, MLA_QK, tm), lambda bi, i: (bi, 0, 0, i)))
        scratch.insert(0, pltpu.VMEM((wuq_t.shape[0], tm), F32))
    return pl.pallas_call(
        functools.partial(_mla_proj_kernel, rope, need_q),
        out_shape=tuple(out_shape), grid=(b, nt), in_specs=in_specs, out_specs=tuple(out_specs),
        scratch_shapes=scratch, compiler_params=_cparams(2), name="mla_proj",
    )(*args)


def _attn_kernel(kv_per_step, n_lat, q_ref, kc_ref, vc_ref, *rest):
    if n_lat:
        kl_ref, vl_ref, o_ref = rest
    else:
        (o_ref,) = rest
    outs = []
    for hh in range(HEADS_PER_STEP):
        kv = hh * kv_per_step // HEADS_PER_STEP
        qt = q_ref[0, hh]
        s = jnp.dot(kc_ref[0, kv, 0], qt, preferred_element_type=F32)
        m = jnp.max(s, axis=0, keepdims=True)
        p = jnp.exp(s - m)
        l = jnp.sum(p, axis=0, keepdims=True)
        acc = jnp.dot(vc_ref[0, kv, 0], p.astype(BF), preferred_element_type=F32)

        def body(kb, carry, kv=kv, qt=qt):
            m, l, acc = carry
            s = jnp.dot(kl_ref[0, kv, kb], qt, preferred_element_type=F32)
            m_new = jnp.maximum(m, jnp.max(s, axis=0, keepdims=True))
            alpha = jnp.exp(m - m_new)
            p = jnp.exp(s - m_new)
            l = alpha * l + jnp.sum(p, axis=0, keepdims=True)
            acc = alpha * acc + jnp.dot(vl_ref[0, kv, kb], p.astype(BF), preferred_element_type=F32)
            return m_new, l, acc

        if n_lat:
            m, l, acc = lax.fori_loop(0, n_lat, body, (m, l, acc))
        outs.append(acc / l)
    o_ref[0] = jnp.concatenate(outs, axis=0).T.astype(BF)


def _attn_call(qt, kc, vct, kl=None, vlt=None):
    b, nh, dq, t = qt.shape
    hkv = kc.shape[1]
    kvps = HEADS_PER_STEP * hkv // nh
    tq = 256
    n_lat = 0 if kl is None else kl.shape[2]
    kvspec = lambda a: pl.BlockSpec((1, kvps) + a.shape[2:], lambda bi, j, i: (bi, j, 0, 0, 0))
    in_specs = [pl.BlockSpec((1, HEADS_PER_STEP, dq, tq), lambda bi, j, i: (bi, j, 0, i)), kvspec(kc), kvspec(vct)]
    args = [qt, kc, vct]
    if n_lat:
        in_specs += [kvspec(kl), kvspec(vlt)]
        args += [kl, vlt]
    return pl.pallas_call(
        functools.partial(_attn_kernel, kvps, n_lat),
        out_shape=jax.ShapeDtypeStruct((b, t, nh * V_DIM), BF),
        grid=(b, nh // HEADS_PER_STEP, t // tq), in_specs=in_specs,
        out_specs=pl.BlockSpec((1, tq, HEADS_PER_STEP * V_DIM), lambda bi, j, i: (bi, i, j)),
        compiler_params=_cparams(3), name="attention",
    )(*args)


def _rope_tables(seq, rot_dim):
    quarter = rot_dim // 4
    pos = jnp.arange(seq)
    row = (pos // GRID_W).astype(F32)
    col = (pos % GRID_W).astype(F32)
    inv = ROPE_THETA ** (-jnp.arange(quarter, dtype=F32) / quarter)
    ang_r = inv[:, None] * row[None, :]
    ang_c = inv[:, None] * col[None, :]
    cos = jnp.concatenate([jnp.cos(ang_r)] * 2 + [jnp.cos(ang_c)] * 2, axis=0)
    sin = jnp.concatenate([-jnp.sin(ang_r), jnp.sin(ang_r), -jnp.sin(ang_c), jnp.sin(ang_c)], axis=0)
    return cos, sin


def _col(v):
    return v.reshape(-1, 1)


def _row(v):
    return v.reshape(1, -1)


def kernel(x, c, ctx, c_ctx, norm1_g, norm2_g, w_mod, b_mod, w_ff1, w_ff2, conv_w_pw1, conv_w_dw, conv_b_dw, conv_ln_g, conv_ln_b, conv_w_pw2, gqa_w_qkv, gqa_q_norm, gqa_k_norm, gqa_w_o, pool_w, pool_scale, mla_w_dq, mla_q_lora_norm, mla_w_uq, mla_w_dkv, mla_kv_lora_norm, mla_w_ukv, mla_q_norm, mla_k_norm, mla_w_o):
    batch, seq, d = x.shape
    depth = w_mod.shape[0]
    n_ctx = ctx.shape[1]

    cvec = jnp.concatenate([c, c_ctx[None], jnp.zeros((8 - batch - 1, d), F32)], axis=0)
    mod = _mod_call(cvec, w_mod, b_mod).reshape(depth, 8, 6, d)
    mod = jnp.pad(mod, ((0, 0), (0, 0), (0, 2), (0, 0)))
    mod_l = mod[:, :batch]
    mod_c = jnp.broadcast_to(mod[:, batch:batch + 1], (depth, batch, 8, d))

    tab_gqa = _rope_tables(seq, GQA_HEAD_DIM)
    tab_mla = _rope_tables(seq, MLA_ROPE)

    def split_kv(k, vt):
        b_, h_, l_, dq_ = k.shape
        n_ = vt.shape[2]
        return k.reshape(b_, h_, n_, l_ // n_, dq_), vt

    xl, xc = x, ctx
    for i in range(depth):
        m = i % N_MIXERS
        j = i // N_MIXERS
        with_ctx = i < depth - 1
        g1 = _row(norm1_g[i])
        g2 = _row(norm2_g[i])
        w1 = w_ff1[i].astype(BF)
        w2 = w_ff2[i].astype(BF)
        ml, mc = mod_l[i], mod_c[i]
        oscale = None
        if m == 0:
            wp1 = conv_w_pw1[j].astype(BF)
            wo = conv_w_pw2[j].astype(BF)
            cargs = (jnp.pad(conv_w_dw[j], ((0, 1), (0, 0))), _row(conv_b_dw[j]), _row(conv_ln_g[j]), _row(conv_ln_b[j]))
            al = _dwconv_call(_glu_call(xl, ml, g1, wp1), *cargs)
            ac = _dwconv_call(_glu_call(xc, mc, g1, wp1), *cargs) if with_ctx else None
        elif m == 1:
            wt = gqa_w_qkv[j].T.astype(BF)
            wo = gqa_w_o[j].astype(BF)
            qg, kg = _col(gqa_q_norm[j]), _col(gqa_k_norm[j])
            ql, kl, vl = _gqa_proj_call(xl, ml, g1, wt, qg, kg, tab_gqa)
            qc, kc, vc = _gqa_proj_call(xc, mc, g1, wt, qg, kg, None)
            kc5, vc5 = split_kv(kc, vc)
            al = _attn_call(ql, kc5, vc5, *split_kv(kl, vl))
            ac = _attn_call(qc, kc5, vc5) if with_ctx else None
        elif m == 2:
            eye = jnp.eye(len(POOL_WINDOWS), dtype=F32)
            wo = jnp.einsum('gcd,gh->gchd', pool_w[j], eye).reshape(d, d).astype(BF)
            oscale = _row(pool_scale[j])
            al = _pool_call(xl, ml, g1)
            ac = _pool_call(xc, mc, g1) if with_ctx else None
        else:
            wo = mla_w_o[j].astype(BF)
            pargs = (mla_w_dq[j].T.astype(BF), _col(mla_q_lora_norm[j]), mla_w_uq[j].T.astype(BF),
                     mla_w_dkv[j].T.astype(BF), _col(mla_kv_lora_norm[j]), mla_w_ukv[j].T.astype(BF),
                     _col(mla_q_norm[j]), _col(mla_k_norm[j]))
            ql, kl, vl = _mla_proj_call(xl, ml, g1, *pargs, tab_mla, True)
            pc = _mla_proj_call(xc, mc, g1, *pargs, None, with_ctx)
            kc5, vc5 = split_kv(*pc[-2:])
            al = _attn_call(ql, kc5, vc5, *split_kv(kl, vl))
            ac = _attn_call(pc[0], kc5, vc5) if with_ctx else None
        xl = _post_call(xl, al, ml, g2, wo, w1, w2, oscale)
        if with_ctx:
            xc = _post_call(xc, ac, mc, g2, wo, w1, w2, oscale)
    return xl
```

```python
import functools

import jax
import jax.numpy as jnp
from jax import lax
from jax.experimental import pallas as pl
from jax.experimental.pallas import tpu as pltpu

D_MODEL = 1024
GRID_W = 64
ROPE_THETA = 10000.0
NORM_EPS = 1e-6
CONV_WIDTH = 31
CONV_PAD = CONV_WIDTH // 2
GQA_HEADS = 16
GQA_KV_HEADS = 4
GQA_HEAD_DIM = 64
POOL_WINDOWS = (2, 4, 8, 16)
POOL_GROUP = 256
MLA_HEADS = 16
MLA_NOPE = 64
MLA_ROPE = 32
MLA_V = 64
MLA_QK = MLA_NOPE + MLA_ROPE
MLA_Q_LORA = 768
MLA_KV_LORA = 256
D_FF = 4 * D_MODEL
N_MIXERS = 4

HALO = 16
HEADS_PER_STEP = 4
V_DIM = 64
VMEM_LIMIT = 52 * 1024 * 1024
LOG2E = 1.4426950408889634
BF = jnp.bfloat16
F32 = jnp.float32


def _cparams(n_axes):
    return pltpu.CompilerParams(dimension_semantics=("arbitrary",) * n_axes, vmem_limit_bytes=VMEM_LIMIT)


def _resident(shape):
    nd = len(shape)
    return pl.BlockSpec(shape, lambda *_: (0,) * nd, pipeline_mode=pl.Buffered(1))


def _norm_mod(x, g, shift, scale):
    ms = jnp.mean(x * x, axis=-1, keepdims=True)
    y = x * lax.rsqrt(ms + NORM_EPS) * g
    return y * (1.0 + scale) + shift


def _rms_rows(xt, gain):
    ms = jnp.mean(xt * xt, axis=0, keepdims=True)
    return xt * lax.rsqrt(ms + NORM_EPS) * gain


def _rope_rows(xt, cos, sin, quarter):
    q = quarter
    rot = jnp.concatenate([xt[q:2 * q], xt[0:q], xt[3 * q:4 * q], xt[2 * q:3 * q]], axis=0)
    return xt * cos + rot * sin


def _mod_kernel(c_ref, w_ref, b_ref, o_ref):
    c = c_ref[...]
    s = (c * jax.nn.sigmoid(c)).astype(BF)
    o_ref[0] = jnp.dot(s, w_ref[0].astype(BF), preferred_element_type=F32) + b_ref[0]


def _mod_call(cvec, w_mod, b_mod):
    depth, d, n = w_mod.shape
    rows = cvec.shape[0]
    tn = 1536
    return pl.pallas_call(
        _mod_kernel,
        out_shape=jax.ShapeDtypeStruct((depth, rows, n), F32),
        grid=(depth, n // tn),
        in_specs=[pl.BlockSpec((rows, d), lambda l, j: (0, 0)),
                  pl.BlockSpec((1, d, tn), lambda l, j: (l, 0, j)),
                  pl.BlockSpec((1, 1, tn), lambda l, j: (l, 0, j))],
        out_specs=pl.BlockSpec((1, rows, tn), lambda l, j: (l, 0, j)),
        compiler_params=_cparams(2), name="adaln_mod",
    )(cvec, w_mod, b_mod.reshape(depth, 1, n))


def _post_kernel(has_oscale, x_ref, a_ref, mod_ref, g2_ref, wo_ref, w1_ref, w2_ref, *rest):
    if has_oscale:
        os_ref, o_ref = rest
    else:
        (o_ref,) = rest
    mod = mod_ref[0]
    y = jnp.dot(a_ref[0], wo_ref[...], preferred_element_type=F32)
    if has_oscale:
        y = y * os_ref[...]
    x1 = x_ref[0] + mod[2:3] * y
    h = _norm_mod(x1, g2_ref[...], mod[3:4], mod[4:5]).astype(BF)
    ffc = D_MODEL
    acc = jnp.zeros_like(x1)
    for c in range(D_FF // ffc):
        t = jnp.dot(h, w1_ref[:, c * ffc:(c + 1) * ffc], preferred_element_type=F32)
        t = jnp.maximum(t, 0.0)
        acc = acc + jnp.dot((t * t).astype(BF), w2_ref[c * ffc:(c + 1) * ffc, :], preferred_element_type=F32)
    o_ref[0] = x1 + mod[5:6] * acc


def _post_call(x, a, mod, g2, wo, w1, w2, oscale=None):
    b, t, d = x.shape
    tm = min(t, 512)
    tile = lambda: pl.BlockSpec((1, tm, d), lambda bi, i: (bi, i, 0))
    in_specs = [tile(), tile(), pl.BlockSpec((1, 8, d), lambda bi, i: (bi, 0, 0)),
                _resident((1, d)), _resident(wo.shape), _resident(w1.shape), _resident(w2.shape)]
    args = [x, a, mod, g2, wo, w1, w2]
    if oscale is not None:
        in_specs.append(_resident((1, d)))
        args.append(oscale)
    return pl.pallas_call(
        functools.partial(_post_kernel, oscale is not None),
        out_shape=jax.ShapeDtypeStruct(x.shape, F32),
        grid=(b, t // tm), in_specs=in_specs, out_specs=tile(),
        compiler_params=_cparams(2), name="outproj_mlp",
    )(*args)


def _glu_kernel(x_ref, mod_ref, g_ref, w_ref, o_ref):
    mod = mod_ref[0]
    h = _norm_mod(x_ref[0], g_ref[...], mod[0:1], mod[1:2]).astype(BF)
    u = jnp.dot(h, w_ref[...], preferred_element_type=F32)
    o_ref[0] = u[:, :D_MODEL] * jax.nn.sigmoid(u[:, D_MODEL:])


def _glu_call(x, mod, g, w):
    b, t, d = x.shape
    tm = min(t, 512)
    tile = lambda: pl.BlockSpec((1, tm, d), lambda bi, i: (bi, i, 0))
    return pl.pallas_call(
        _glu_kernel, out_shape=jax.ShapeDtypeStruct(x.shape, F32), grid=(b, t // tm),
        in_specs=[tile(), pl.BlockSpec((1, 8, d), lambda bi, i: (bi, 0, 0)), _resident((1, d)), _resident(w.shape)],
        out_specs=tile(), compiler_params=_cparams(2), name="conv_pw1_glu",
    )(x, mod, g, w)


def _halo_specs(t, tm, d):
    r = tm // HALO
    last = t // HALO - 1
    return [pl.BlockSpec((1, tm, d), lambda bi, i: (bi, i, 0)),
            pl.BlockSpec((1, HALO, d), lambda bi, i: (bi, jnp.maximum(i * r - 1, 0), 0)),
            pl.BlockSpec((1, HALO, d), lambda bi, i: (bi, jnp.minimum((i + 1) * r, last), 0))]


def _dwconv_kernel(tm, u_ref, up_ref, un_ref, w_ref, b_ref, lg_ref, lb_ref, o_ref, win_ref, y_ref):
    i = pl.program_id(1)
    n = pl.num_programs(1)
    win_ref[0:HALO] = jnp.where(i > 0, up_ref[0], 0.0)
    win_ref[HALO:HALO + tm] = u_ref[0]
    win_ref[HALO + tm:] = jnp.where(i < n - 1, un_ref[0], 0.0)
    rc, cc = 64, 256
    for c0 in range(0, D_MODEL, cc):
        for r0 in range(0, tm, rc):
            acc = jnp.broadcast_to(b_ref[:, c0:c0 + cc], (rc, cc))
            for j in range(CONV_WIDTH):
                acc = acc + win_ref[pl.ds(r0 + j + HALO - CONV_PAD, rc), c0:c0 + cc] * w_ref[pl.ds(j, 1), c0:c0 + cc]
            y_ref[r0:r0 + rc, c0:c0 + cc] = acc
    y = y_ref[...]
    mu = jnp.mean(y, axis=-1, keepdims=True)
    yc = y - mu
    var = jnp.mean(yc * yc, axis=-1, keepdims=True)
    z = yc * lax.rsqrt(var + NORM_EPS) * lg_ref[...] + lb_ref[...]
    o_ref[0] = (z * jax.nn.sigmoid(z)).astype(BF)


def _dwconv_call(u, w_dw, b_dw, ln_g, ln_b):
    b, t, d = u.shape
    tm = 128
    return pl.pallas_call(
        functools.partial(_dwconv_kernel, tm),
        out_shape=jax.ShapeDtypeStruct(u.shape, BF), grid=(b, t // tm),
        in_specs=_halo_specs(t, tm, d) + [_resident(w_dw.shape), _resident((1, d)), _resident((1, d)), _resident((1, d))],
        out_specs=pl.BlockSpec((1, tm, d), lambda bi, i: (bi, i, 0)),
        scratch_shapes=[pltpu.VMEM((tm + 2 * HALO, d), F32), pltpu.VMEM((tm, d), F32)],
        compiler_params=_cparams(2), name="dwconv_ln_silu",
    )(u, u, u, w_dw, b_dw, ln_g, ln_b)


def _pool_kernel(tm, t_total, x_ref, xp_ref, xn_ref, mod_ref, g_ref, o_ref, win_ref):
    i = pl.program_id(1)
    n = pl.num_programs(1)
    mod = mod_ref[0]
    nm = lambda v: _norm_mod(v, g_ref[...], mod[0:1], mod[1:2])
    win_ref[0:HALO] = jnp.where(i > 0, nm(xp_ref[0]), 0.0)
    win_ref[HALO:HALO + tm] = nm(x_ref[0])
    win_ref[HALO + tm:] = jnp.where(i < n - 1, nm(xn_ref[0]), 0.0)
    pos = i * tm + lax.broadcasted_iota(jnp.int32, (tm, 1), 0)
    for gi, w in enumerate(POOL_WINDOWS):
        c0 = gi * POOL_GROUP
        half = w // 2
        acc = win_ref[pl.ds(HALO - half, tm), c0:c0 + POOL_GROUP]
        for o in range(-half + 1, w - half):
            acc = acc + win_ref[pl.ds(HALO + o, tm), c0:c0 + POOL_GROUP]
        lo = jnp.maximum(pos - half, 0)
        hi = jnp.minimum(pos + (w - half), t_total)
        cnt = (hi - lo).astype(F32)
        o_ref[0, :, c0:c0 + POOL_GROUP] = (acc / cnt - win_ref[HALO:HALO + tm, c0:c0 + POOL_GROUP]).astype(BF)


def _pool_call(x, mod, g):
    b, t, d = x.shape
    tm = 256
    return pl.pallas_call(
        functools.partial(_pool_kernel, tm, t),
        out_shape=jax.ShapeDtypeStruct(x.shape, BF), grid=(b, t // tm),
        in_specs=_halo_specs(t, tm, d) + [pl.BlockSpec((1, 8, d), lambda bi, i: (bi, 0, 0)), _resident((1, d))],
        out_specs=pl.BlockSpec((1, tm, d), lambda bi, i: (bi, i, 0)),
        scratch_shapes=[pltpu.VMEM((tm + 2 * HALO, d), F32)],
        compiler_params=_cparams(2), name="pool_windows",
    )(x, x, x, mod, g)


def _gqa_proj_kernel(rope, x_ref, mod_ref, g_ref, wt_ref, qg_ref, kg_ref, *rest):
    if rope:
        cos_ref, sin_ref, q_ref, k_ref, v_ref, y_ref = rest
    else:
        q_ref, k_ref, v_ref, y_ref = rest
    mod = mod_ref[0]
    h = _norm_mod(x_ref[0], g_ref[...], mod[0:1], mod[1:2]).astype(BF)
    y_ref[...] = lax.dot_general(wt_ref[...], h, (((1,), (1,)), ((), ())), preferred_element_type=F32)
    hd = GQA_HEAD_DIM
    nq = GQA_HEADS * hd
    nk = GQA_KV_HEADS * hd

    def head(row0, gain_ref):
        z = _rms_rows(y_ref[row0:row0 + hd], gain_ref[...])
        if rope:
            z = _rope_rows(z, cos_ref[...], sin_ref[...], hd // 4)
        return z

    for hh in range(GQA_HEADS):
        q_ref[0, hh] = (head(hh * hd, qg_ref) * (hd ** -0.5 * LOG2E)).astype(BF)
    for hh in range(GQA_KV_HEADS):
        k_ref[0, hh] = head(nq + hh * hd, kg_ref).T.astype(BF)
        v_ref[0, hh, 0] = y_ref[nq + nk + hh * hd:nq + nk + (hh + 1) * hd].astype(BF)


def _gqa_proj_call(x, mod, g, wt, qg, kg, tables):
    b, t, d = x.shape
    tm = min(t, 512)
    nt = t // tm
    hd = GQA_HEAD_DIM
    rope = tables is not None
    in_specs = [pl.BlockSpec((1, tm, d), lambda bi, i: (bi, i, 0)), pl.BlockSpec((1, 8, d), lambda bi, i: (bi, 0, 0)),
                _resident((1, d)), _resident(wt.shape), _resident((hd, 1)), _resident((hd, 1))]
    args = [x, mod, g, wt, qg, kg]
    if rope:
        in_specs += [pl.BlockSpec((hd, tm), lambda bi, i: (0, i))] * 2
        args += list(tables)
    return pl.pallas_call(
        functools.partial(_gqa_proj_kernel, rope),
        out_shape=(jax.ShapeDtypeStruct((b, GQA_HEADS, hd, t), BF),
                   jax.ShapeDtypeStruct((b, GQA_KV_HEADS, t, hd), BF),
                   jax.ShapeDtypeStruct((b, GQA_KV_HEADS, nt, hd, tm), BF)),
        grid=(b, nt), in_specs=in_specs,
        out_specs=(pl.BlockSpec((1, GQA_HEADS, hd, tm), lambda bi, i: (bi, 0, 0, i)),
                   pl.BlockSpec((1, GQA_KV_HEADS, tm, hd), lambda bi, i: (bi, 0, i, 0)),
                   pl.BlockSpec((1, GQA_KV_HEADS, 1, hd, tm), lambda bi, i: (bi, 0, i, 0, 0))),
        scratch_shapes=[pltpu.VMEM((wt.shape[0], tm), F32)],
        compiler_params=_cparams(2), name="gqa_qkv_proj",
    )(*args)


def _mla_proj_kernel(rope, need_q, x_ref, mod_ref, g_ref, wdq_ref, qln_ref, wuq_ref, wdkv_ref, kvln_ref, wukv_ref,
                     qg_ref, kg_ref, *rest):
    rest = list(rest)
    if rope:
        cos_ref, sin_ref = rest[:2]
        rest = rest[2:]
    if need_q:
        q_ref, k_ref, v_ref, yq_ref, ykv_ref = rest
    else:
        k_ref, v_ref, ykv_ref = rest
    mod = mod_ref[0]
    nt_dims = (((1,), (1,)), ((), ()))
    h = _norm_mod(x_ref[0], g_ref[...], mod[0:1], mod[1:2]).astype(BF)

    def tail_rope(z):
        if not rope:
            return z
        return jnp.concatenate([z[:MLA_NOPE], _rope_rows(z[MLA_NOPE:], cos_ref[...], sin_ref[...], MLA_ROPE // 4)], axis=0)

    if need_q:
        cq = lax.dot_general(wdq_ref[...], h, nt_dims, preferred_element_type=F32)
        cq = _rms_rows(cq, qln_ref[...]).astype(BF)
        yq_ref[...] = jnp.dot(wuq_ref[...], cq, preferred_element_type=F32)
        for hh in range(MLA_HEADS):
            z = tail_rope(_rms_rows(yq_ref[hh * MLA_QK:(hh + 1) * MLA_QK], qg_ref[...]))
            q_ref[0, hh] = (z * (MLA_QK ** -0.5 * LOG2E)).astype(BF)
    dkv = lax.dot_general(wdkv_ref[...], h, nt_dims, preferred_element_type=F32)
    ckv = _rms_rows(dkv[:MLA_KV_LORA], kvln_ref[...]).astype(BF)
    k_rope = dkv[MLA_KV_LORA:]
    ykv_ref[...] = jnp.dot(wukv_ref[...], ckv, preferred_element_type=F32)
    per = MLA_NOPE + MLA_V
    for hh in range(MLA_HEADS):
        kf = jnp.concatenate([ykv_ref[hh * per:hh * per + MLA_NOPE], k_rope], axis=0)
        k_ref[0, hh] = tail_rope(_rms_rows(kf, kg_ref[...])).T.astype(BF)
        v_ref[0, hh, 0] = ykv_ref[hh * per + MLA_NOPE:(hh + 1) * per].astype(BF)


def _mla_proj_call(x, mod, g, wdq_t, qln, wuq_t, wdkv_t, kvln, wukv_t, qg, kg, tables, need_q):
    b, t, d = x.shape
    tm = min(t, 512)
    nt = t // tm
    rope = tables is not None
    in_specs = [pl.BlockSpec((1, tm, d), lambda bi, i: (bi, i, 0)), pl.BlockSpec((1, 8, d), lambda bi, i: (bi, 0, 0)),
                _resident((1, d)), _resident(wdq_t.shape), _resident(qln.shape), _resident(wuq_t.shape),
                _resident(wdkv_t.shape), _resident(kvln.shape), _resident(wukv_t.shape),
                _resident(qg.shape), _resident(kg.shape)]
    args = [x, mod, g, wdq_t, qln, wuq_t, wdkv_t, kvln, wukv_t, qg, kg]
    if rope:
        in_specs += [pl.BlockSpec((MLA_ROPE, tm), lambda bi, i: (0, i))] * 2
        args += list(tables)
    out_shape = [jax.ShapeDtypeStruct((b, MLA_HEADS, t, MLA_QK), BF),
                 jax.ShapeDtypeStruct((b, MLA_HEADS, nt, MLA_V, tm), BF)]
    out_specs = [pl.BlockSpec((1, MLA_HEADS, tm, MLA_QK), lambda bi, i: (bi, 0, i, 0)),
                 pl.BlockSpec((1, MLA_HEADS, 1, MLA_V, tm), lambda bi, i: (bi, 0, i, 0, 0))]
    scratch = [pltpu.VMEM((wukv_t.shape[0], tm), F32)]
    if need_q:
        out_shape.insert(0, jax.ShapeDtypeStruct((b, MLA_HEADS, MLA_QK, t), BF))
        out_specs.insert(0, pl.BlockSpec((1, MLA_HEADS, MLA_QK, tm), lambda bi, i: (bi, 0, 0, i)))
        scratch.insert(0, pltpu.VMEM((wuq_t.shape[0], tm), F32))
    return pl.pallas_call(
        functools.partial(_mla_proj_kernel, rope, need_q),
        out_shape=tuple(out_shape), grid=(b, nt), in_specs=in_specs, out_specs=tuple(out_specs),
        scratch_shapes=scratch, compiler_params=_cparams(2), name="mla_proj",
    )(*args)


def _attn_kernel(kv_per_step, n_lat, q_ref, kc_ref, vc_ref, *rest):
    if n_lat:
        kl_ref, vl_ref, o_ref, *scratch = rest
    else:
        o_ref, *scratch = rest
    m_ref, l_ref, acc_ref, s0, s1, p0, p1, a0, a1 = scratch
    s_buf, p_buf, a_buf = (s0, s1), (p0, p1), (a0, a1)
    heads = range(HEADS_PER_STEP)
    kv_of = lambda hh: hh * kv_per_step // HEADS_PER_STEP
    lc = kc_ref.shape[3]
    tk = kl_ref.shape[3] if n_lat else 0

    def stage_s(c, n):
        rows = lc if n is None else tk
        for hh in heads:
            k = kc_ref[0, kv_of(hh), 0] if n is None else kl_ref[0, kv_of(hh), n]
            s_buf[c][hh, 0:rows] = jnp.dot(k, q_ref[0, hh], preferred_element_type=F32)

    def stage_sm(c, rows):
        for hh in heads:
            s = s_buf[c][hh, 0:rows]
            m_old = m_ref[hh]
            m_new = jnp.maximum(m_old, jnp.max(s, axis=0, keepdims=True))
            alpha = jnp.exp2(m_old - m_new)
            p = jnp.exp2(s - m_new)
            m_ref[hh] = m_new
            l_ref[hh] = alpha * l_ref[hh] + jnp.sum(p, axis=0, keepdims=True)
            a_buf[c][hh] = alpha
            p_buf[c][hh, 0:rows] = p.astype(BF)

    def stage_pv(c, n):
        rows = lc if n is None else tk
        for hh in heads:
            v = vc_ref[0, kv_of(hh), 0] if n is None else vl_ref[0, kv_of(hh), n]
            pv = jnp.dot(v, p_buf[c][hh, 0:rows], preferred_element_type=F32)
            acc_ref[hh] = a_buf[c][hh] * acc_ref[hh] + pv

    m_ref[...] = jnp.full(m_ref.shape, -1e30, F32)
    l_ref[...] = jnp.zeros(l_ref.shape, F32)
    acc_ref[...] = jnp.zeros(acc_ref.shape, F32)
    stage_s(0, None)
    if n_lat:
        stage_s(1, 0)
        stage_sm(0, lc)
        stage_s(0, 1)
        stage_sm(1, tk)
        stage_pv(0, None)

        def pair(it, carry):
            j = 1 + 2 * it
            stage_s(1, j + 1)
            stage_sm(0, tk)
            stage_pv(1, j - 1)
            stage_s(0, j + 2)
            stage_sm(1, tk)
            stage_pv(0, j)
            return carry

        lax.fori_loop(0, (n_lat - 2) // 2, pair, 0)
        stage_sm(0, tk)
        stage_pv(1, n_lat - 2)
        stage_pv(0, n_lat - 1)
    else:
        stage_sm(0, lc)
        stage_pv(0, None)
    outs = [acc_ref[hh] / l_ref[hh] for hh in heads]
    o_ref[0] = jnp.concatenate(outs, axis=0).T.astype(BF)


def _attn_call(qt, kc, vct, kl=None, vlt=None):
    b, nh, dq, t = qt.shape
    hkv = kc.shape[1]
    kvps = HEADS_PER_STEP * hkv // nh
    tq = 256
    n_lat = 0 if kl is None else kl.shape[2]
    assert n_lat % 2 == 0, "the skewed key-block loop handles latent blocks in pairs"
    rows_max = kc.shape[3] if kl is None else max(kc.shape[3], kl.shape[3])
    kvspec = lambda a: pl.BlockSpec((1, kvps) + a.shape[2:], lambda bi, j, i: (bi, j, 0, 0, 0))
    in_specs = [pl.BlockSpec((1, HEADS_PER_STEP, dq, tq), lambda bi, j, i: (bi, j, 0, i)), kvspec(kc), kvspec(vct)]
    args = [qt, kc, vct]
    if n_lat:
        in_specs += [kvspec(kl), kvspec(vlt)]
        args += [kl, vlt]
    return pl.pallas_call(
        functools.partial(_attn_kernel, kvps, n_lat),
        out_shape=jax.ShapeDtypeStruct((b, t, nh * V_DIM), BF),
        grid=(b, nh // HEADS_PER_STEP, t // tq), in_specs=in_specs,
        out_specs=pl.BlockSpec((1, tq, HEADS_PER_STEP * V_DIM), lambda bi, j, i: (bi, i, j)),
        scratch_shapes=[pltpu.VMEM((HEADS_PER_STEP, 1, tq), F32), pltpu.VMEM((HEADS_PER_STEP, 1, tq), F32),
                        pltpu.VMEM((HEADS_PER_STEP, V_DIM, tq), F32)]
        + [pltpu.VMEM((HEADS_PER_STEP, rows_max, tq), F32)] * 2
        + [pltpu.VMEM((HEADS_PER_STEP, rows_max, tq), BF)] * 2
        + [pltpu.VMEM((HEADS_PER_STEP, 1, tq), F32)] * 2,
        compiler_params=_cparams(3), name="attention",
    )(*args)


def _rope_tables(seq, rot_dim):
    quarter = rot_dim // 4
    pos = jnp.arange(seq)
    row = (pos // GRID_W).astype(F32)
    col = (pos % GRID_W).astype(F32)
    inv = ROPE_THETA ** (-jnp.arange(quarter, dtype=F32) / quarter)
    ang_r = inv[:, None] * row[None, :]
    ang_c = inv[:, None] * col[None, :]
    cos = jnp.concatenate([jnp.cos(ang_r)] * 2 + [jnp.cos(ang_c)] * 2, axis=0)
    sin = jnp.concatenate([-jnp.sin(ang_r), jnp.sin(ang_r), -jnp.sin(ang_c), jnp.sin(ang_c)], axis=0)
    return cos, sin


def _col(v):
    return v.reshape(-1, 1)


def _row(v):
    return v.reshape(1, -1)


def kernel(x, c, ctx, c_ctx, norm1_g, norm2_g, w_mod, b_mod, w_ff1, w_ff2, conv_w_pw1, conv_w_dw, conv_b_dw, conv_ln_g, conv_ln_b, conv_w_pw2, gqa_w_qkv, gqa_q_norm, gqa_k_norm, gqa_w_o, pool_w, pool_scale, mla_w_dq, mla_q_lora_norm, mla_w_uq, mla_w_dkv, mla_kv_lora_norm, mla_w_ukv, mla_q_norm, mla_k_norm, mla_w_o):
    batch, seq, d = x.shape
    depth = w_mod.shape[0]
    n_ctx = ctx.shape[1]

    cvec = jnp.concatenate([c, c_ctx[None], jnp.zeros((8 - batch - 1, d), F32)], axis=0)
    mod = _mod_call(cvec, w_mod, b_mod).reshape(depth, 8, 6, d)
    mod = jnp.pad(mod, ((0, 0), (0, 0), (0, 2), (0, 0)))
    mod_l = mod[:, :batch]
    mod_c = jnp.broadcast_to(mod[:, batch:batch + 1], (depth, batch, 8, d))

    tab_gqa = _rope_tables(seq, GQA_HEAD_DIM)
    tab_mla = _rope_tables(seq, MLA_ROPE)

    def split_kv(k, vt):
        b_, h_, l_, dq_ = k.shape
        n_ = vt.shape[2]
        return k.reshape(b_, h_, n_, l_ // n_, dq_), vt

    xl, xc = x, ctx
    for i in range(depth):
        m = i % N_MIXERS
        j = i // N_MIXERS
        with_ctx = i < depth - 1
        g1 = _row(norm1_g[i])
        g2 = _row(norm2_g[i])
        w1 = w_ff1[i].astype(BF)
        w2 = w_ff2[i].astype(BF)
        ml, mc = mod_l[i], mod_c[i]
        oscale = None
        if m == 0:
            wp1 = conv_w_pw1[j].astype(BF)
            wo = conv_w_pw2[j].astype(BF)
            cargs = (jnp.pad(conv_w_dw[j], ((0, 1), (0, 0))), _row(conv_b_dw[j]), _row(conv_ln_g[j]), _row(conv_ln_b[j]))
            al = _dwconv_call(_glu_call(xl, ml, g1, wp1), *cargs)
            ac = _dwconv_call(_glu_call(xc, mc, g1, wp1), *cargs) if with_ctx else None
        elif m == 1:
            wt = gqa_w_qkv[j].T.astype(BF)
            wo = gqa_w_o[j].astype(BF)
            qg, kg = _col(gqa_q_norm[j]), _col(gqa_k_norm[j])
            ql, kl, vl = _gqa_proj_call(xl, ml, g1, wt, qg, kg, tab_gqa)
            qc, kc, vc = _gqa_proj_call(xc, mc, g1, wt, qg, kg, None)
            kc5, vc5 = split_kv(kc, vc)
            al = _attn_call(ql, kc5, vc5, *split_kv(kl, vl))
            ac = _attn_call(qc, kc5, vc5) if with_ctx else None
        elif m == 2:
            eye = jnp.eye(len(POOL_WINDOWS), dtype=F32)
            wo = jnp.einsum('gcd,gh->gchd', pool_w[j], eye).reshape(d, d).astype(BF)
            oscale = _row(pool_scale[j])
            al = _pool_call(xl, ml, g1)
            ac = _pool_call(xc, mc, g1) if with_ctx else None
        else:
            wo = mla_w_o[j].astype(BF)
            pargs = (mla_w_dq[j].T.astype(BF), _col(mla_q_lora_norm[j]), mla_w_uq[j].T.astype(BF),
                     mla_w_dkv[j].T.astype(BF), _col(mla_kv_lora_norm[j]), mla_w_ukv[j].T.astype(BF),
                     _col(mla_q_norm[j]), _col(mla_k_norm[j]))
            ql, kl, vl = _mla_proj_call(xl, ml, g1, *pargs, tab_mla, True)
            pc = _mla_proj_call(xc, mc, g1, *pargs, None, with_ctx)
            kc5, vc5 = split_kv(*pc[-2:])
            al = _attn_call(ql, kc5, vc5, *split_kv(kl, vl))
            ac = _attn_call(pc[0], kc5, vc5) if with_ctx else None
        xl = _post_call(xl, al, ml, g2, wo, w1, w2, oscale)
        if with_ctx:
            xc = _post_call(xc, ac, mc, g2, wo, w1, w2, oscale)
    return xl
```

```python
import functools

import jax
import jax.numpy as jnp
from jax import lax
from jax.experimental import pallas as pl
from jax.experimental.pallas import tpu as pltpu

D_MODEL = 1024
GRID_W = 64
ROPE_THETA = 10000.0
NORM_EPS = 1e-6
CONV_WIDTH = 31
CONV_PAD = CONV_WIDTH // 2
GQA_HEADS = 16
GQA_KV_HEADS = 4
GQA_HEAD_DIM = 64
POOL_WINDOWS = (2, 4, 8, 16)
POOL_GROUP = 256
MLA_HEADS = 16
MLA_NOPE = 64
MLA_ROPE = 32
MLA_V = 64
MLA_QK = MLA_NOPE + MLA_ROPE
MLA_Q_LORA = 768
MLA_KV_LORA = 256
D_FF = 4 * D_MODEL
N_MIXERS = 4

HALO = 16
HEADS_PER_STEP = 4
V_DIM = 64
V_ROWS = V_DIM + 16
VMEM_LIMIT = 52 * 1024 * 1024
LOG2E = 1.4426950408889634
BF = jnp.bfloat16
F32 = jnp.float32


def _cparams(n_axes):
    return pltpu.CompilerParams(dimension_semantics=("arbitrary",) * n_axes, vmem_limit_bytes=VMEM_LIMIT)


def _resident(shape):
    nd = len(shape)
    return pl.BlockSpec(shape, lambda *_: (0,) * nd, pipeline_mode=pl.Buffered(1))


def _norm_mod(x, g, shift, scale):
    ms = jnp.mean(x * x, axis=-1, keepdims=True)
    y = x * lax.rsqrt(ms + NORM_EPS) * g
    return y * (1.0 + scale) + shift


def _rms_rows(xt, gain):
    ms = jnp.mean(xt * xt, axis=0, keepdims=True)
    return xt * lax.rsqrt(ms + NORM_EPS) * gain


def _rope_rows(xt, cos, sin, quarter):
    q = quarter
    rot = jnp.concatenate([xt[q:2 * q], xt[0:q], xt[3 * q:4 * q], xt[2 * q:3 * q]], axis=0)
    return xt * cos + rot * sin


def _mod_kernel(c_ref, w_ref, b_ref, o_ref):
    c = c_ref[...]
    s = (c * jax.nn.sigmoid(c)).astype(BF)
    o_ref[0] = jnp.dot(s, w_ref[0].astype(BF), preferred_element_type=F32) + b_ref[0]


def _mod_call(cvec, w_mod, b_mod):
    depth, d, n = w_mod.shape
    rows = cvec.shape[0]
    tn = 1536
    return pl.pallas_call(
        _mod_kernel,
        out_shape=jax.ShapeDtypeStruct((depth, rows, n), F32),
        grid=(depth, n // tn),
        in_specs=[pl.BlockSpec((rows, d), lambda l, j: (0, 0)),
                  pl.BlockSpec((1, d, tn), lambda l, j: (l, 0, j)),
                  pl.BlockSpec((1, 1, tn), lambda l, j: (l, 0, j))],
        out_specs=pl.BlockSpec((1, rows, tn), lambda l, j: (l, 0, j)),
        compiler_params=_cparams(2), name="adaln_mod",
    )(cvec, w_mod, b_mod.reshape(depth, 1, n))


def _post_kernel(has_oscale, x_ref, a_ref, mod_ref, g2_ref, wo_ref, w1_ref, w2_ref, *rest):
    if has_oscale:
        os_ref, o_ref = rest
    else:
        (o_ref,) = rest
    mod = mod_ref[0]
    y = jnp.dot(a_ref[0], wo_ref[...], preferred_element_type=F32)
    if has_oscale:
        y = y * os_ref[...]
    x1 = x_ref[0] + mod[2:3] * y
    h = _norm_mod(x1, g2_ref[...], mod[3:4], mod[4:5]).astype(BF)
    ffc = D_MODEL
    acc = jnp.zeros_like(x1)
    for c in range(D_FF // ffc):
        t = jnp.dot(h, w1_ref[:, c * ffc:(c + 1) * ffc], preferred_element_type=F32)
        t = jnp.maximum(t, 0.0)
        acc = acc + jnp.dot((t * t).astype(BF), w2_ref[c * ffc:(c + 1) * ffc, :], preferred_element_type=F32)
    o_ref[0] = x1 + mod[5:6] * acc


def _post_call(x, a, mod, g2, wo, w1, w2, oscale=None):
    b, t, d = x.shape
    tm = min(t, 512)
    tile = lambda: pl.BlockSpec((1, tm, d), lambda bi, i: (bi, i, 0))
    in_specs = [tile(), tile(), pl.BlockSpec((1, 8, d), lambda bi, i: (bi, 0, 0)),
                _resident((1, d)), _resident(wo.shape), _resident(w1.shape), _resident(w2.shape)]
    args = [x, a, mod, g2, wo, w1, w2]
    if oscale is not None:
        in_specs.append(_resident((1, d)))
        args.append(oscale)
    return pl.pallas_call(
        functools.partial(_post_kernel, oscale is not None),
        out_shape=jax.ShapeDtypeStruct(x.shape, F32),
        grid=(b, t // tm), in_specs=in_specs, out_specs=tile(),
        compiler_params=_cparams(2), name="outproj_mlp",
    )(*args)


def _glu_kernel(x_ref, mod_ref, g_ref, w_ref, o_ref):
    mod = mod_ref[0]
    h = _norm_mod(x_ref[0], g_ref[...], mod[0:1], mod[1:2]).astype(BF)
    u = jnp.dot(h, w_ref[...], preferred_element_type=F32)
    o_ref[0] = u[:, :D_MODEL] * jax.nn.sigmoid(u[:, D_MODEL:])


def _glu_call(x, mod, g, w):
    b, t, d = x.shape
    tm = min(t, 512)
    tile = lambda: pl.BlockSpec((1, tm, d), lambda bi, i: (bi, i, 0))
    return pl.pallas_call(
        _glu_kernel, out_shape=jax.ShapeDtypeStruct(x.shape, F32), grid=(b, t // tm),
        in_specs=[tile(), pl.BlockSpec((1, 8, d), lambda bi, i: (bi, 0, 0)), _resident((1, d)), _resident(w.shape)],
        out_specs=tile(), compiler_params=_cparams(2), name="conv_pw1_glu",
    )(x, mod, g, w)


def _halo_specs(t, tm, d):
    r = tm // HALO
    last = t // HALO - 1
    return [pl.BlockSpec((1, tm, d), lambda bi, i: (bi, i, 0)),
            pl.BlockSpec((1, HALO, d), lambda bi, i: (bi, jnp.maximum(i * r - 1, 0), 0)),
            pl.BlockSpec((1, HALO, d), lambda bi, i: (bi, jnp.minimum((i + 1) * r, last), 0))]


def _dwconv_kernel(tm, u_ref, up_ref, un_ref, w_ref, b_ref, lg_ref, lb_ref, o_ref, win_ref, y_ref):
    i = pl.program_id(1)
    n = pl.num_programs(1)
    win_ref[0:HALO] = jnp.where(i > 0, up_ref[0], 0.0)
    win_ref[HALO:HALO + tm] = u_ref[0]
    win_ref[HALO + tm:] = jnp.where(i < n - 1, un_ref[0], 0.0)
    rc, cc = 128, 128
    span = rc + 2 * HALO
    for c0 in range(0, D_MODEL, cc):
        for r0 in range(0, tm, rc):
            x = win_ref[r0:r0 + span, c0:c0 + cc]
            acc = jnp.broadcast_to(b_ref[:, c0:c0 + cc], (rc, cc))
            for sub in range(8):
                xs = x if sub == 0 else pltpu.roll(x, span - sub, axis=0)
                for off in range(sub, 2 * HALO, 8):
                    j = off - (HALO - CONV_PAD)
                    if 0 <= j < CONV_WIDTH:
                        acc = acc + xs[off - sub:off - sub + rc] * w_ref[pl.ds(j, 1), c0:c0 + cc]
            y_ref[r0:r0 + rc, c0:c0 + cc] = acc
    y = y_ref[...]
    mu = jnp.mean(y, axis=-1, keepdims=True)
    yc = y - mu
    var = jnp.mean(yc * yc, axis=-1, keepdims=True)
    z = yc * lax.rsqrt(var + NORM_EPS) * lg_ref[...] + lb_ref[...]
    o_ref[0] = (z * jax.nn.sigmoid(z)).astype(BF)


def _dwconv_call(u, w_dw, b_dw, ln_g, ln_b):
    b, t, d = u.shape
    tm = 128
    return pl.pallas_call(
        functools.partial(_dwconv_kernel, tm),
        out_shape=jax.ShapeDtypeStruct(u.shape, BF), grid=(b, t // tm),
        in_specs=_halo_specs(t, tm, d) + [_resident(w_dw.shape), _resident((1, d)), _resident((1, d)), _resident((1, d))],
        out_specs=pl.BlockSpec((1, tm, d), lambda bi, i: (bi, i, 0)),
        scratch_shapes=[pltpu.VMEM((tm + 2 * HALO, d), F32), pltpu.VMEM((tm, d), F32)],
        compiler_params=_cparams(2), name="dwconv_ln_silu",
    )(u, u, u, w_dw, b_dw, ln_g, ln_b)


def _pool_kernel(tm, t_total, x_ref, xp_ref, xn_ref, mod_ref, g_ref, o_ref, win_ref):
    i = pl.program_id(1)
    n = pl.num_programs(1)
    mod = mod_ref[0]
    nm = lambda v: _norm_mod(v, g_ref[...], mod[0:1], mod[1:2])
    win_ref[0:HALO] = jnp.where(i > 0, nm(xp_ref[0]), 0.0)
    win_ref[HALO:HALO + tm] = nm(x_ref[0])
    win_ref[HALO + tm:] = jnp.where(i < n - 1, nm(xn_ref[0]), 0.0)
    pos = i * tm + lax.broadcasted_iota(jnp.int32, (tm, 1), 0)
    for gi, w in enumerate(POOL_WINDOWS):
        c0 = gi * POOL_GROUP
        half = w // 2
        acc = win_ref[pl.ds(HALO - half, tm), c0:c0 + POOL_GROUP]
        for o in range(-half + 1, w - half):
            acc = acc + win_ref[pl.ds(HALO + o, tm), c0:c0 + POOL_GROUP]
        lo = jnp.maximum(pos - half, 0)
        hi = jnp.minimum(pos + (w - half), t_total)
        cnt = (hi - lo).astype(F32)
        o_ref[0, :, c0:c0 + POOL_GROUP] = (acc / cnt - win_ref[HALO:HALO + tm, c0:c0 + POOL_GROUP]).astype(BF)


def _pool_call(x, mod, g):
    b, t, d = x.shape
    tm = 256
    return pl.pallas_call(
        functools.partial(_pool_kernel, tm, t),
        out_shape=jax.ShapeDtypeStruct(x.shape, BF), grid=(b, t // tm),
        in_specs=_halo_specs(t, tm, d) + [pl.BlockSpec((1, 8, d), lambda bi, i: (bi, 0, 0)), _resident((1, d))],
        out_specs=pl.BlockSpec((1, tm, d), lambda bi, i: (bi, i, 0)),
        scratch_shapes=[pltpu.VMEM((tm + 2 * HALO, d), F32)],
        compiler_params=_cparams(2), name="pool_windows",
    )(x, x, x, mod, g)


def _gqa_proj_kernel(rope, x_ref, mod_ref, g_ref, wt_ref, qg_ref, kg_ref, *rest):
    if rope:
        cos_ref, sin_ref, q_ref, k_ref, v_ref, y_ref = rest
    else:
        q_ref, k_ref, v_ref, y_ref = rest
    mod = mod_ref[0]
    h = _norm_mod(x_ref[0], g_ref[...], mod[0:1], mod[1:2]).astype(BF)
    y_ref[...] = lax.dot_general(wt_ref[...], h, (((1,), (1,)), ((), ())), preferred_element_type=F32)
    hd = GQA_HEAD_DIM
    nq = GQA_HEADS * hd
    nk = GQA_KV_HEADS * hd

    def head(row0, gain_ref):
        z = _rms_rows(y_ref[row0:row0 + hd], gain_ref[...])
        if rope:
            z = _rope_rows(z, cos_ref[...], sin_ref[...], hd // 4)
        return z

    for hh in range(GQA_HEADS):
        q_ref[0, hh] = (head(hh * hd, qg_ref) * (hd ** -0.5 * LOG2E)).astype(BF)
    for hh in range(GQA_KV_HEADS):
        k_ref[0, hh] = head(nq + hh * hd, kg_ref).T.astype(BF)
        v_ref[0, hh, 0, 0:V_DIM] = y_ref[nq + nk + hh * hd:nq + nk + (hh + 1) * hd].astype(BF)
        v_ref[0, hh, 0, V_DIM:V_ROWS] = jnp.ones((V_ROWS - V_DIM, v_ref.shape[-1]), BF)


def _gqa_proj_call(x, mod, g, wt, qg, kg, tables):
    b, t, d = x.shape
    tm = min(t, 512)
    nt = t // tm
    hd = GQA_HEAD_DIM
    rope = tables is not None
    in_specs = [pl.BlockSpec((1, tm, d), lambda bi, i: (bi, i, 0)), pl.BlockSpec((1, 8, d), lambda bi, i: (bi, 0, 0)),
                _resident((1, d)), _resident(wt.shape), _resident((hd, 1)), _resident((hd, 1))]
    args = [x, mod, g, wt, qg, kg]
    if rope:
        in_specs += [pl.BlockSpec((hd, tm), lambda bi, i: (0, i))] * 2
        args += list(tables)
    return pl.pallas_call(
        functools.partial(_gqa_proj_kernel, rope),
        out_shape=(jax.ShapeDtypeStruct((b, GQA_HEADS, hd, t), BF),
                   jax.ShapeDtypeStruct((b, GQA_KV_HEADS, t, hd), BF),
                   jax.ShapeDtypeStruct((b, GQA_KV_HEADS, nt, V_ROWS, tm), BF)),
        grid=(b, nt), in_specs=in_specs,
        out_specs=(pl.BlockSpec((1, GQA_HEADS, hd, tm), lambda bi, i: (bi, 0, 0, i)),
                   pl.BlockSpec((1, GQA_KV_HEADS, tm, hd), lambda bi, i: (bi, 0, i, 0)),
                   pl.BlockSpec((1, GQA_KV_HEADS, 1, V_ROWS, tm), lambda bi, i: (bi, 0, i, 0, 0))),
        scratch_shapes=[pltpu.VMEM((wt.shape[0], tm), F32)],
        compiler_params=_cparams(2), name="gqa_qkv_proj",
    )(*args)


def _mla_proj_kernel(rope, need_q, x_ref, mod_ref, g_ref, wdq_ref, qln_ref, wuq_ref, wdkv_ref, kvln_ref, wukv_ref,
                     qg_ref, kg_ref, *rest):
    rest = list(rest)
    if rope:
        cos_ref, sin_ref = rest[:2]
        rest = rest[2:]
    if need_q:
        q_ref, k_ref, v_ref, yq_ref, ykv_ref = rest
    else:
        k_ref, v_ref, ykv_ref = rest
    mod = mod_ref[0]
    nt_dims = (((1,), (1,)), ((), ()))
    h = _norm_mod(x_ref[0], g_ref[...], mod[0:1], mod[1:2]).astype(BF)

    def tail_rope(z):
        if not rope:
            return z
        return jnp.concatenate([z[:MLA_NOPE], _rope_rows(z[MLA_NOPE:], cos_ref[...], sin_ref[...], MLA_ROPE // 4)], axis=0)

    if need_q:
        cq = lax.dot_general(wdq_ref[...], h, nt_dims, preferred_element_type=F32)
        cq = _rms_rows(cq, qln_ref[...]).astype(BF)
        yq_ref[...] = jnp.dot(wuq_ref[...], cq, preferred_element_type=F32)
        for hh in range(MLA_HEADS):
            z = tail_rope(_rms_rows(yq_ref[hh * MLA_QK:(hh + 1) * MLA_QK], qg_ref[...]))
            q_ref[0, hh] = (z * (MLA_QK ** -0.5 * LOG2E)).astype(BF)
    dkv = lax.dot_general(wdkv_ref[...], h, nt_dims, preferred_element_type=F32)
    ckv = _rms_rows(dkv[:MLA_KV_LORA], kvln_ref[...]).astype(BF)
    k_rope = dkv[MLA_KV_LORA:]
    ykv_ref[...] = jnp.dot(wukv_ref[...], ckv, preferred_element_type=F32)
    per = MLA_NOPE + MLA_V
    for hh in range(MLA_HEADS):
        kf = jnp.concatenate([ykv_ref[hh * per:hh * per + MLA_NOPE], k_rope], axis=0)
        k_ref[0, hh] = tail_rope(_rms_rows(kf, kg_ref[...])).T.astype(BF)
        v_ref[0, hh, 0, 0:V_DIM] = ykv_ref[hh * per + MLA_NOPE:(hh + 1) * per].astype(BF)
        v_ref[0, hh, 0, V_DIM:V_ROWS] = jnp.ones((V_ROWS - V_DIM, v_ref.shape[-1]), BF)


def _mla_proj_call(x, mod, g, wdq_t, qln, wuq_t, wdkv_t, kvln, wukv_t, qg, kg, tables, need_q):
    b, t, d = x.shape
    tm = min(t, 512)
    nt = t // tm
    rope = tables is not None
    in_specs = [pl.BlockSpec((1, tm, d), lambda bi, i: (bi, i, 0)), pl.BlockSpec((1, 8, d), lambda bi, i: (bi, 0, 0)),
                _resident((1, d)), _resident(wdq_t.shape), _resident(qln.shape), _resident(wuq_t.shape),
                _resident(wdkv_t.shape), _resident(kvln.shape), _resident(wukv_t.shape),
                _resident(qg.shape), _resident(kg.shape)]
    args = [x, mod, g, wdq_t, qln, wuq_t, wdkv_t, kvln, wukv_t, qg, kg]
    if rope:
        in_specs += [pl.BlockSpec((MLA_ROPE, tm), lambda bi, i: (0, i))] * 2
        args += list(tables)
    out_shape = [jax.ShapeDtypeStruct((b, MLA_HEADS, t, MLA_QK), BF),
                 jax.ShapeDtypeStruct((b, MLA_HEADS, nt, V_ROWS, tm), BF)]
    out_specs = [pl.BlockSpec((1, MLA_HEADS, tm, MLA_QK), lambda bi, i: (bi, 0, i, 0)),
                 pl.BlockSpec((1, MLA_HEADS, 1, V_ROWS, tm), lambda bi, i: (bi, 0, i, 0, 0))]
    scratch = [pltpu.VMEM((wukv_t.shape[0], tm), F32)]
    if need_q:
        out_shape.insert(0, jax.ShapeDtypeStruct((b, MLA_HEADS, MLA_QK, t), BF))
        out_specs.insert(0, pl.BlockSpec((1, MLA_HEADS, MLA_QK, tm), lambda bi, i: (bi, 0, 0, i)))
        scratch.insert(0, pltpu.VMEM((wuq_t.shape[0], tm), F32))
    return pl.pallas_call(
        functools.partial(_mla_proj_kernel, rope, need_q),
        out_shape=tuple(out_shape), grid=(b, nt), in_specs=in_specs, out_specs=tuple(out_specs),
        scratch_shapes=scratch, compiler_params=_cparams(2), name="mla_proj",
    )(*args)


def _attn_kernel(kv_per_step, n_lat, q_ref, kc_ref, vc_ref, *rest):
    if n_lat:
        kl_ref, vl_ref, o_ref, *scratch = rest
    else:
        o_ref, *scratch = rest
    m_ref, acc_ref, s0, s1, p0, p1, a0, a1, x0, x1 = scratch
    s_buf, p_buf, a_buf, x_buf = (s0, s1), (p0, p1), (a0, a1), (x0, x1)
    heads = range(HEADS_PER_STEP)
    kv_of = lambda hh: hh * kv_per_step // HEADS_PER_STEP
    lc = kc_ref.shape[3]
    tk = kl_ref.shape[3] if n_lat else 0

    def stage_s(c, n):
        rows = lc if n is None else tk
        for hh in heads:
            k = kc_ref[0, kv_of(hh), 0] if n is None else kl_ref[0, kv_of(hh), n]
            s = jnp.dot(k, q_ref[0, hh], preferred_element_type=F32)
            s_buf[c][hh, 0:rows] = s
            x_buf[c][hh] = jnp.max(s, axis=0, keepdims=True)

    def stage_sm(c, rows):
        for hh in heads:
            m_old = m_ref[hh]
            m_new = jnp.maximum(m_old, x_buf[c][hh])
            m_ref[hh] = m_new
            a_buf[c][hh] = jnp.exp2(m_old - m_new)
            p_buf[c][hh, 0:rows] = jnp.exp2(s_buf[c][hh, 0:rows] - m_new).astype(BF)

    def stage_pv(c, n):
        rows = lc if n is None else tk
        for hh in heads:
            v = vc_ref[0, kv_of(hh), 0] if n is None else vl_ref[0, kv_of(hh), n]
            pv = jnp.dot(v, p_buf[c][hh, 0:rows], preferred_element_type=F32)
            acc_ref[hh] = a_buf[c][hh] * acc_ref[hh] + pv

    m_ref[...] = jnp.full(m_ref.shape, -1e30, F32)
    acc_ref[...] = jnp.zeros(acc_ref.shape, F32)
    stage_s(0, None)
    if n_lat:
        stage_s(1, 0)
        stage_sm(0, lc)
        stage_s(0, 1)
        stage_sm(1, tk)
        stage_pv(0, None)

        def pair(it, carry):
            j = 1 + 2 * it
            stage_s(1, j + 1)
            stage_sm(0, tk)
            stage_pv(1, j - 1)
            stage_s(0, j + 2)
            stage_sm(1, tk)
            stage_pv(0, j)
            return carry

        lax.fori_loop(0, (n_lat - 2) // 2, pair, 0)
        stage_sm(0, tk)
        stage_pv(1, n_lat - 2)
        stage_pv(0, n_lat - 1)
    else:
        stage_sm(0, lc)
        stage_pv(0, None)
    outs = [acc_ref[hh, 0:V_DIM] / acc_ref[hh, V_DIM:V_DIM + 1] for hh in heads]
    o_ref[0] = jnp.concatenate(outs, axis=0).T.astype(BF)


def _attn_call(qt, kc, vct, kl=None, vlt=None):
    b, nh, dq, t = qt.shape
    hkv = kc.shape[1]
    kvps = HEADS_PER_STEP * hkv // nh
    tq = 256
    n_lat = 0 if kl is None else kl.shape[2]
    assert n_lat % 2 == 0, "the skewed key-block loop handles latent blocks in pairs"
    rows_max = kc.shape[3] if kl is None else max(kc.shape[3], kl.shape[3])
    kvspec = lambda a: pl.BlockSpec((1, kvps) + a.shape[2:], lambda bi, j, i: (bi, j, 0, 0, 0))
    in_specs = [pl.BlockSpec((1, HEADS_PER_STEP, dq, tq), lambda bi, j, i: (bi, j, 0, i)), kvspec(kc), kvspec(vct)]
    args = [qt, kc, vct]
    if n_lat:
        in_specs += [kvspec(kl), kvspec(vlt)]
        args += [kl, vlt]
    return pl.pallas_call(
        functools.partial(_attn_kernel, kvps, n_lat),
        out_shape=jax.ShapeDtypeStruct((b, t, nh * V_DIM), BF),
        grid=(b, nh // HEADS_PER_STEP, t // tq), in_specs=in_specs,
        out_specs=pl.BlockSpec((1, tq, HEADS_PER_STEP * V_DIM), lambda bi, j, i: (bi, i, j)),
        scratch_shapes=[pltpu.VMEM((HEADS_PER_STEP, 1, tq), F32), pltpu.VMEM((HEADS_PER_STEP, V_ROWS, tq), F32)]
        + [pltpu.VMEM((HEADS_PER_STEP, rows_max, tq), F32)] * 2
        + [pltpu.VMEM((HEADS_PER_STEP, rows_max, tq), BF)] * 2
        + [pltpu.VMEM((HEADS_PER_STEP, 1, tq), F32)] * 4,
        compiler_params=_cparams(3), name="attention",
    )(*args)


def _rope_tables(seq, rot_dim):
    quarter = rot_dim // 4
    pos = jnp.arange(seq)
    row = (pos // GRID_W).astype(F32)
    col = (pos % GRID_W).astype(F32)
    inv = ROPE_THETA ** (-jnp.arange(quarter, dtype=F32) / quarter)
    ang_r = inv[:, None] * row[None, :]
    ang_c = inv[:, None] * col[None, :]
    cos = jnp.concatenate([jnp.cos(ang_r)] * 2 + [jnp.cos(ang_c)] * 2, axis=0)
    sin = jnp.concatenate([-jnp.sin(ang_r), jnp.sin(ang_r), -jnp.sin(ang_c), jnp.sin(ang_c)], axis=0)
    return cos, sin


def _col(v):
    return v.reshape(-1, 1)


def _row(v):
    return v.reshape(1, -1)


def kernel(x, c, ctx, c_ctx, norm1_g, norm2_g, w_mod, b_mod, w_ff1, w_ff2, conv_w_pw1, conv_w_dw, conv_b_dw, conv_ln_g, conv_ln_b, conv_w_pw2, gqa_w_qkv, gqa_q_norm, gqa_k_norm, gqa_w_o, pool_w, pool_scale, mla_w_dq, mla_q_lora_norm, mla_w_uq, mla_w_dkv, mla_kv_lora_norm, mla_w_ukv, mla_q_norm, mla_k_norm, mla_w_o):
    batch, seq, d = x.shape
    depth = w_mod.shape[0]
    n_ctx = ctx.shape[1]

    cvec = jnp.concatenate([c, c_ctx[None], jnp.zeros((8 - batch - 1, d), F32)], axis=0)
    mod = _mod_call(cvec, w_mod, b_mod).reshape(depth, 8, 6, d)
    mod = jnp.pad(mod, ((0, 0), (0, 0), (0, 2), (0, 0)))
    mod_l = mod[:, :batch]
    mod_c = jnp.broadcast_to(mod[:, batch:batch + 1], (depth, batch, 8, d))

    tab_gqa = _rope_tables(seq, GQA_HEAD_DIM)
    tab_mla = _rope_tables(seq, MLA_ROPE)

    def split_kv(k, vt):
        b_, h_, l_, dq_ = k.shape
        n_ = vt.shape[2]
        return k.reshape(b_, h_, n_, l_ // n_, dq_), vt

    xl, xc = x, ctx
    for i in range(depth):
        m = i % N_MIXERS
        j = i // N_MIXERS
        with_ctx = i < depth - 1
        g1 = _row(norm1_g[i])
        g2 = _row(norm2_g[i])
        w1 = w_ff1[i].astype(BF)
        w2 = w_ff2[i].astype(BF)
        ml, mc = mod_l[i], mod_c[i]
        oscale = None
        if m == 0:
            wp1 = conv_w_pw1[j].astype(BF)
            wo = conv_w_pw2[j].astype(BF)
            cargs = (jnp.pad(conv_w_dw[j], ((0, 1), (0, 0))), _row(conv_b_dw[j]), _row(conv_ln_g[j]), _row(conv_ln_b[j]))
            al = _dwconv_call(_glu_call(xl, ml, g1, wp1), *cargs)
            ac = _dwconv_call(_glu_call(xc, mc, g1, wp1), *cargs) if with_ctx else None
        elif m == 1:
            wt = gqa_w_qkv[j].T.astype(BF)
            wo = gqa_w_o[j].astype(BF)
            qg, kg = _col(gqa_q_norm[j]), _col(gqa_k_norm[j])
            ql, kl, vl = _gqa_proj_call(xl, ml, g1, wt, qg, kg, tab_gqa)
            qc, kc, vc = _gqa_proj_call(xc, mc, g1, wt, qg, kg, None)
            kc5, vc5 = split_kv(kc, vc)
            al = _attn_call(ql, kc5, vc5, *split_kv(kl, vl))
            ac = _attn_call(qc, kc5, vc5) if with_ctx else None
        elif m == 2:
            eye = jnp.eye(len(POOL_WINDOWS), dtype=F32)
            wo = jnp.einsum('gcd,gh->gchd', pool_w[j], eye).reshape(d, d).astype(BF)
            oscale = _row(pool_scale[j])
            al = _pool_call(xl, ml, g1)
            ac = _pool_call(xc, mc, g1) if with_ctx else None
        else:
            wo = mla_w_o[j].astype(BF)
            pargs = (mla_w_dq[j].T.astype(BF), _col(mla_q_lora_norm[j]), mla_w_uq[j].T.astype(BF),
                     mla_w_dkv[j].T.astype(BF), _col(mla_kv_lora_norm[j]), mla_w_ukv[j].T.astype(BF),
                     _col(mla_q_norm[j]), _col(mla_k_norm[j]))
            ql, kl, vl = _mla_proj_call(xl, ml, g1, *pargs, tab_mla, True)
            pc = _mla_proj_call(xc, mc, g1, *pargs, None, with_ctx)
            kc5, vc5 = split_kv(*pc[-2:])
            al = _attn_call(ql, kc5, vc5, *split_kv(kl, vl))
            ac = _attn_call(pc[0], kc5, vc5) if with_ctx else None
        xl = _post_call(xl, al, ml, g2, wo, w1, w2, oscale)
        if with_ctx:
            xc = _post_call(xc, ac, mc, g2, wo, w1, w2, oscale)
    return xl
```

```python
import functools

import jax
import jax.numpy as jnp
from jax import lax
from jax.experimental import pallas as pl
from jax.experimental.pallas import tpu as pltpu

D_MODEL = 1024
GRID_W = 64
ROPE_THETA = 10000.0
NORM_EPS = 1e-6
CONV_WIDTH = 31
CONV_PAD = CONV_WIDTH // 2
GQA_HEADS = 16
GQA_KV_HEADS = 4
GQA_HEAD_DIM = 64
POOL_WINDOWS = (2, 4, 8, 16)
POOL_GROUP = 256
MLA_HEADS = 16
MLA_NOPE = 64
MLA_ROPE = 32
MLA_V = 64
MLA_QK = MLA_NOPE + MLA_ROPE
MLA_Q_LORA = 768
MLA_KV_LORA = 256
D_FF = 4 * D_MODEL
N_MIXERS = 4

HALO = 16
HEADS_PER_STEP = 4
V_DIM = 64
PROJ_ROW_CHUNK = 512
ATTN_TILES_PER_STEP = 4
V_ROWS = V_DIM + 16
VMEM_LIMIT = 52 * 1024 * 1024
LOG2E = 1.4426950408889634
BF = jnp.bfloat16
F32 = jnp.float32


def _cparams(n_axes):
    return pltpu.CompilerParams(dimension_semantics=("arbitrary",) * n_axes, vmem_limit_bytes=VMEM_LIMIT)


def _resident(shape):
    nd = len(shape)
    return pl.BlockSpec(shape, lambda *_: (0,) * nd, pipeline_mode=pl.Buffered(1))


def _norm_mod(x, g, shift, scale):
    ms = jnp.mean(x * x, axis=-1, keepdims=True)
    y = x * lax.rsqrt(ms + NORM_EPS) * g
    return y * (1.0 + scale) + shift


def _rms_rows(xt, gain):
    ms = jnp.mean(xt * xt, axis=0, keepdims=True)
    return xt * lax.rsqrt(ms + NORM_EPS) * gain


def _rope_rows(xt, cos, sin, quarter):
    q = quarter
    rot = jnp.concatenate([xt[q:2 * q], xt[0:q], xt[3 * q:4 * q], xt[2 * q:3 * q]], axis=0)
    return xt * cos + rot * sin


def _mod_kernel(c_ref, w_ref, b_ref, o_ref):
    c = c_ref[...]
    s = (c * jax.nn.sigmoid(c)).astype(BF)
    o_ref[0] = jnp.dot(s, w_ref[0].astype(BF), preferred_element_type=F32) + b_ref[0]


def _mod_call(cvec, w_mod, b_mod):
    depth, d, n = w_mod.shape
    rows = cvec.shape[0]
    tn = 1536
    return pl.pallas_call(
        _mod_kernel,
        out_shape=jax.ShapeDtypeStruct((depth, rows, n), F32),
        grid=(depth, n // tn),
        in_specs=[pl.BlockSpec((rows, d), lambda l, j: (0, 0)),
                  pl.BlockSpec((1, d, tn), lambda l, j: (l, 0, j)),
                  pl.BlockSpec((1, 1, tn), lambda l, j: (l, 0, j))],
        out_specs=pl.BlockSpec((1, rows, tn), lambda l, j: (l, 0, j)),
        compiler_params=_cparams(2), name="adaln_mod",
    )(cvec, w_mod, b_mod.reshape(depth, 1, n))


def _post_kernel(has_oscale, x_ref, a_ref, mod_ref, g2_ref, wo_ref, w1_ref, w2_ref, *rest):
    if has_oscale:
        os_ref, o_ref = rest
    else:
        (o_ref,) = rest
    mod = mod_ref[0]
    y = jnp.dot(a_ref[0], wo_ref[...], preferred_element_type=F32)
    if has_oscale:
        y = y * os_ref[...]
    x1 = x_ref[0] + mod[2:3] * y
    h = _norm_mod(x1, g2_ref[...], mod[3:4], mod[4:5]).astype(BF)
    ffc = D_MODEL
    acc = jnp.zeros_like(x1)
    for c in range(D_FF // ffc):
        t = jnp.dot(h, w1_ref[0, :, c * ffc:(c + 1) * ffc], preferred_element_type=F32)
        t = jnp.maximum(t, 0.0)
        acc = acc + jnp.dot((t * t).astype(BF), w2_ref[0, c * ffc:(c + 1) * ffc, :], preferred_element_type=F32)
    o_ref[0] = x1 + mod[5:6] * acc


def _post_call(x, a, mod, g2, wo, w1, w2, layer, oscale=None):
    b, t, d = x.shape
    tm = min(t, 512)
    tile = lambda: pl.BlockSpec((1, tm, d), lambda bi, i: (bi, i, 0))
    layer_w = lambda w: pl.BlockSpec((1,) + w.shape[1:], lambda bi, i: (layer, 0, 0), pipeline_mode=pl.Buffered(1))
    in_specs = [tile(), tile(), pl.BlockSpec((1, 8, d), lambda bi, i: (bi, 0, 0)),
                _resident((1, d)), _resident(wo.shape), layer_w(w1), layer_w(w2)]
    args = [x, a, mod, g2, wo, w1, w2]
    if oscale is not None:
        in_specs.append(_resident((1, d)))
        args.append(oscale)
    return pl.pallas_call(
        functools.partial(_post_kernel, oscale is not None),
        out_shape=jax.ShapeDtypeStruct(x.shape, F32),
        grid=(b, t // tm), in_specs=in_specs, out_specs=tile(),
        compiler_params=_cparams(2), name="outproj_mlp",
    )(*args)


def _glu_kernel(x_ref, mod_ref, g_ref, w_ref, o_ref):
    mod = mod_ref[0]
    h = _norm_mod(x_ref[0], g_ref[...], mod[0:1], mod[1:2]).astype(BF)
    u = jnp.dot(h, w_ref[...], preferred_element_type=F32)
    o_ref[0] = u[:, :D_MODEL] * jax.nn.sigmoid(u[:, D_MODEL:])


def _glu_call(x, mod, g, w):
    b, t, d = x.shape
    tm = min(t, 512)
    tile = lambda: pl.BlockSpec((1, tm, d), lambda bi, i: (bi, i, 0))
    return pl.pallas_call(
        _glu_kernel, out_shape=jax.ShapeDtypeStruct(x.shape, F32), grid=(b, t // tm),
        in_specs=[tile(), pl.BlockSpec((1, 8, d), lambda bi, i: (bi, 0, 0)), _resident((1, d)), _resident(w.shape)],
        out_specs=tile(), compiler_params=_cparams(2), name="conv_pw1_glu",
    )(x, mod, g, w)


def _halo_specs(t, tm, d):
    r = tm // HALO
    last = t // HALO - 1
    return [pl.BlockSpec((1, tm, d), lambda bi, i: (bi, i, 0)),
            pl.BlockSpec((1, HALO, d), lambda bi, i: (bi, jnp.maximum(i * r - 1, 0), 0)),
            pl.BlockSpec((1, HALO, d), lambda bi, i: (bi, jnp.minimum((i + 1) * r, last), 0))]


def _dwconv_kernel(tm, u_ref, up_ref, un_ref, w_ref, b_ref, lg_ref, lb_ref, o_ref, win_ref, y_ref):
    i = pl.program_id(1)
    n = pl.num_programs(1)
    win_ref[0:HALO] = jnp.where(i > 0, up_ref[0], 0.0)
    win_ref[HALO:HALO + tm] = u_ref[0]
    win_ref[HALO + tm:] = jnp.where(i < n - 1, un_ref[0], 0.0)
    rc, cc = 128, 128
    span = rc + 2 * HALO
    for c0 in range(0, D_MODEL, cc):
        for r0 in range(0, tm, rc):
            x = win_ref[r0:r0 + span, c0:c0 + cc]
            acc = jnp.broadcast_to(b_ref[:, c0:c0 + cc], (rc, cc))
            for sub in range(8):
                xs = x if sub == 0 else pltpu.roll(x, span - sub, axis=0)
                for off in range(sub, 2 * HALO, 8):
                    j = off - (HALO - CONV_PAD)
                    if 0 <= j < CONV_WIDTH:
                        acc = acc + xs[off - sub:off - sub + rc] * w_ref[pl.ds(j, 1), c0:c0 + cc]
            y_ref[r0:r0 + rc, c0:c0 + cc] = acc
    y = y_ref[...]
    mu = jnp.mean(y, axis=-1, keepdims=True)
    yc = y - mu
    var = jnp.mean(yc * yc, axis=-1, keepdims=True)
    z = yc * lax.rsqrt(var + NORM_EPS) * lg_ref[...] + lb_ref[...]
    o_ref[0] = (z * jax.nn.sigmoid(z)).astype(BF)


def _dwconv_call(u, w_dw, b_dw, ln_g, ln_b):
    b, t, d = u.shape
    tm = 128
    return pl.pallas_call(
        functools.partial(_dwconv_kernel, tm),
        out_shape=jax.ShapeDtypeStruct(u.shape, BF), grid=(b, t // tm),
        in_specs=_halo_specs(t, tm, d) + [_resident(w_dw.shape), _resident((1, d)), _resident((1, d)), _resident((1, d))],
        out_specs=pl.BlockSpec((1, tm, d), lambda bi, i: (bi, i, 0)),
        scratch_shapes=[pltpu.VMEM((tm + 2 * HALO, d), F32), pltpu.VMEM((tm, d), F32)],
        compiler_params=_cparams(2), name="dwconv_ln_silu",
    )(u, u, u, w_dw, b_dw, ln_g, ln_b)


def _pool_kernel(tm, t_total, x_ref, xp_ref, xn_ref, mod_ref, g_ref, o_ref, win_ref):
    i = pl.program_id(1)
    n = pl.num_programs(1)
    mod = mod_ref[0]
    nm = lambda v: _norm_mod(v, g_ref[...], mod[0:1], mod[1:2])
    win_ref[0:HALO] = jnp.where(i > 0, nm(xp_ref[0]), 0.0)
    win_ref[HALO:HALO + tm] = nm(x_ref[0])
    win_ref[HALO + tm:] = jnp.where(i < n - 1, nm(xn_ref[0]), 0.0)
    pos = i * tm + lax.broadcasted_iota(jnp.int32, (tm, 1), 0)
    for gi, w in enumerate(POOL_WINDOWS):
        c0 = gi * POOL_GROUP
        half = w // 2
        acc = win_ref[pl.ds(HALO - half, tm), c0:c0 + POOL_GROUP]
        for o in range(-half + 1, w - half):
            acc = acc + win_ref[pl.ds(HALO + o, tm), c0:c0 + POOL_GROUP]
        lo = jnp.maximum(pos - half, 0)
        hi = jnp.minimum(pos + (w - half), t_total)
        cnt = (hi - lo).astype(F32)
        o_ref[0, :, c0:c0 + POOL_GROUP] = (acc / cnt - win_ref[HALO:HALO + tm, c0:c0 + POOL_GROUP]).astype(BF)


def _pool_call(x, mod, g):
    b, t, d = x.shape
    tm = 256
    return pl.pallas_call(
        functools.partial(_pool_kernel, tm, t),
        out_shape=jax.ShapeDtypeStruct(x.shape, BF), grid=(b, t // tm),
        in_specs=_halo_specs(t, tm, d) + [pl.BlockSpec((1, 8, d), lambda bi, i: (bi, 0, 0)), _resident((1, d))],
        out_specs=pl.BlockSpec((1, tm, d), lambda bi, i: (bi, i, 0)),
        scratch_shapes=[pltpu.VMEM((tm + 2 * HALO, d), F32)],
        compiler_params=_cparams(2), name="pool_windows",
    )(x, x, x, mod, g)


def _gqa_proj_kernel(rope, x_ref, mod_ref, g_ref, wt_ref, qg_ref, kg_ref, *rest):
    if rope:
        cos_ref, sin_ref, q_ref, k_ref, v_ref, y_ref = rest
    else:
        q_ref, k_ref, v_ref, y_ref = rest
    mod = mod_ref[0]
    h = _norm_mod(x_ref[0], g_ref[...], mod[0:1], mod[1:2]).astype(BF)
    for r0 in range(0, y_ref.shape[0], PROJ_ROW_CHUNK):
        y_ref[r0:r0 + PROJ_ROW_CHUNK] = lax.dot_general(wt_ref[r0:r0 + PROJ_ROW_CHUNK], h, (((1,), (1,)), ((), ())),
                                                        preferred_element_type=F32)
    hd = GQA_HEAD_DIM
    nq = GQA_HEADS * hd
    nk = GQA_KV_HEADS * hd

    def head(row0, gain_ref):
        z = _rms_rows(y_ref[row0:row0 + hd], gain_ref[...])
        if rope:
            z = _rope_rows(z, cos_ref[...], sin_ref[...], hd // 4)
        return z

    for hh in range(GQA_HEADS):
        q_ref[0, hh] = (head(hh * hd, qg_ref) * (hd ** -0.5 * LOG2E)).astype(BF)
    for hh in range(GQA_KV_HEADS):
        k_ref[0, hh] = head(nq + hh * hd, kg_ref).T.astype(BF)
        v_ref[0, hh, 0, 0:V_DIM] = y_ref[nq + nk + hh * hd:nq + nk + (hh + 1) * hd].astype(BF)
        v_ref[0, hh, 0, V_DIM:V_ROWS] = jnp.ones((V_ROWS - V_DIM, v_ref.shape[-1]), BF)


def _gqa_proj_call(x, mod, g, wt, qg, kg, tables):
    b, t, d = x.shape
    tm = min(t, 512)
    nt = t // tm
    hd = GQA_HEAD_DIM
    rope = tables is not None
    in_specs = [pl.BlockSpec((1, tm, d), lambda bi, i: (bi, i, 0)), pl.BlockSpec((1, 8, d), lambda bi, i: (bi, 0, 0)),
                _resident((1, d)), _resident(wt.shape), _resident((hd, 1)), _resident((hd, 1))]
    args = [x, mod, g, wt, qg, kg]
    if rope:
        in_specs += [pl.BlockSpec((hd, tm), lambda bi, i: (0, i))] * 2
        args += list(tables)
    return pl.pallas_call(
        functools.partial(_gqa_proj_kernel, rope),
        out_shape=(jax.ShapeDtypeStruct((b, GQA_HEADS, hd, t), BF),
                   jax.ShapeDtypeStruct((b, GQA_KV_HEADS, t, hd), BF),
                   jax.ShapeDtypeStruct((b, GQA_KV_HEADS, nt, V_ROWS, tm), BF)),
        grid=(b, nt), in_specs=in_specs,
        out_specs=(pl.BlockSpec((1, GQA_HEADS, hd, tm), lambda bi, i: (bi, 0, 0, i)),
                   pl.BlockSpec((1, GQA_KV_HEADS, tm, hd), lambda bi, i: (bi, 0, i, 0)),
                   pl.BlockSpec((1, GQA_KV_HEADS, 1, V_ROWS, tm), lambda bi, i: (bi, 0, i, 0, 0))),
        scratch_shapes=[pltpu.VMEM((wt.shape[0], tm), F32)],
        compiler_params=_cparams(2), name="gqa_qkv_proj",
    )(*args)


def _mla_proj_kernel(rope, need_q, x_ref, mod_ref, g_ref, wdq_ref, qln_ref, wuq_ref, wdkv_ref, kvln_ref, wukv_ref,
                     qg_ref, kg_ref, *rest):
    rest = list(rest)
    if rope:
        cos_ref, sin_ref = rest[:2]
        rest = rest[2:]
    if need_q:
        q_ref, k_ref, v_ref, yq_ref, ykv_ref = rest
    else:
        k_ref, v_ref, ykv_ref = rest
    mod = mod_ref[0]
    nt_dims = (((1,), (1,)), ((), ()))
    h = _norm_mod(x_ref[0], g_ref[...], mod[0:1], mod[1:2]).astype(BF)

    def tail_rope(z):
        if not rope:
            return z
        return jnp.concatenate([z[:MLA_NOPE], _rope_rows(z[MLA_NOPE:], cos_ref[...], sin_ref[...], MLA_ROPE // 4)], axis=0)

    if need_q:
        cq = lax.dot_general(wdq_ref[...], h, nt_dims, preferred_element_type=F32)
        cq = _rms_rows(cq, qln_ref[...]).astype(BF)
        for r0 in range(0, yq_ref.shape[0], PROJ_ROW_CHUNK):
            yq_ref[r0:r0 + PROJ_ROW_CHUNK] = jnp.dot(wuq_ref[r0:r0 + PROJ_ROW_CHUNK], cq, preferred_element_type=F32)
        for hh in range(MLA_HEADS):
            z = tail_rope(_rms_rows(yq_ref[hh * MLA_QK:(hh + 1) * MLA_QK], qg_ref[...]))
            q_ref[0, hh] = (z * (MLA_QK ** -0.5 * LOG2E)).astype(BF)
    dkv = lax.dot_general(wdkv_ref[...], h, nt_dims, preferred_element_type=F32)
    ckv = _rms_rows(dkv[:MLA_KV_LORA], kvln_ref[...]).astype(BF)
    k_rope = dkv[MLA_KV_LORA:]
    for r0 in range(0, ykv_ref.shape[0], PROJ_ROW_CHUNK):
        ykv_ref[r0:r0 + PROJ_ROW_CHUNK] = jnp.dot(wukv_ref[r0:r0 + PROJ_ROW_CHUNK], ckv, preferred_element_type=F32)
    per = MLA_NOPE + MLA_V
    for hh in range(MLA_HEADS):
        kf = jnp.concatenate([ykv_ref[hh * per:hh * per + MLA_NOPE], k_rope], axis=0)
        k_ref[0, hh] = tail_rope(_rms_rows(kf, kg_ref[...])).T.astype(BF)
        v_ref[0, hh, 0, 0:V_DIM] = ykv_ref[hh * per + MLA_NOPE:(hh + 1) * per].astype(BF)
        v_ref[0, hh, 0, V_DIM:V_ROWS] = jnp.ones((V_ROWS - V_DIM, v_ref.shape[-1]), BF)


def _mla_proj_call(x, mod, g, wdq_t, qln, wuq_t, wdkv_t, kvln, wukv_t, qg, kg, tables, need_q):
    b, t, d = x.shape
    tm = min(t, 512)
    nt = t // tm
    rope = tables is not None
    in_specs = [pl.BlockSpec((1, tm, d), lambda bi, i: (bi, i, 0)), pl.BlockSpec((1, 8, d), lambda bi, i: (bi, 0, 0)),
                _resident((1, d)), _resident(wdq_t.shape), _resident(qln.shape), _resident(wuq_t.shape),
                _resident(wdkv_t.shape), _resident(kvln.shape), _resident(wukv_t.shape),
                _resident(qg.shape), _resident(kg.shape)]
    args = [x, mod, g, wdq_t, qln, wuq_t, wdkv_t, kvln, wukv_t, qg, kg]
    if rope:
        in_specs += [pl.BlockSpec((MLA_ROPE, tm), lambda bi, i: (0, i))] * 2
        args += list(tables)
    out_shape = [jax.ShapeDtypeStruct((b, MLA_HEADS, t, MLA_QK), BF),
                 jax.ShapeDtypeStruct((b, MLA_HEADS, nt, V_ROWS, tm), BF)]
    out_specs = [pl.BlockSpec((1, MLA_HEADS, tm, MLA_QK), lambda bi, i: (bi, 0, i, 0)),
                 pl.BlockSpec((1, MLA_HEADS, 1, V_ROWS, tm), lambda bi, i: (bi, 0, i, 0, 0))]
    scratch = [pltpu.VMEM((wukv_t.shape[0], tm), F32)]
    if need_q:
        out_shape.insert(0, jax.ShapeDtypeStruct((b, MLA_HEADS, MLA_QK, t), BF))
        out_specs.insert(0, pl.BlockSpec((1, MLA_HEADS, MLA_QK, tm), lambda bi, i: (bi, 0, 0, i)))
        scratch.insert(0, pltpu.VMEM((wuq_t.shape[0], tm), F32))
    return pl.pallas_call(
        functools.partial(_mla_proj_kernel, rope, need_q),
        out_shape=tuple(out_shape), grid=(b, nt), in_specs=in_specs, out_specs=tuple(out_specs),
        scratch_shapes=scratch, compiler_params=_cparams(2), name="mla_proj",
    )(*args)


def _attn_kernel(kv_per_step, n_lat, tq, q_ref, kc_ref, vc_ref, *rest):
    if n_lat:
        kl_ref, vl_ref, o_ref, m_ref, acc_ref, *bufs = rest
    else:
        o_ref, m_ref, acc_ref, *bufs = rest
    heads = range(HEADS_PER_STEP)
    kv_of = lambda hh: hh * kv_per_step // HEADS_PER_STEP
    lc = kc_ref.shape[3]
    tk = kl_ref.shape[3] if n_lat else 0
    n_tiles = q_ref.shape[3] // tq

    for ti in range(n_tiles):
        s0, s1, p0, p1, a0, a1, x0, x1 = bufs[8 * (ti % 2):8 * (ti % 2 + 1)]
        s_buf, p_buf, a_buf, x_buf = (s0, s1), (p0, p1), (a0, a1), (x0, x1)
        cols = slice(ti * tq, (ti + 1) * tq)
        st = lambda hh, ti=ti: ti * HEADS_PER_STEP + hh

        def stage_s(c, n, s_buf=s_buf, x_buf=x_buf, cols=cols):
            rows = lc if n is None else tk
            for hh in heads:
                k = kc_ref[0, kv_of(hh), 0] if n is None else kl_ref[0, kv_of(hh), n]
                s = jnp.dot(k, q_ref[0, hh, :, cols], preferred_element_type=F32)
                s_buf[c][hh, 0:rows] = s
                x_buf[c][hh] = jnp.max(s, axis=0, keepdims=True)

        def stage_sm(c, rows, s_buf=s_buf, p_buf=p_buf, a_buf=a_buf, x_buf=x_buf, st=st):
            for hh in heads:
                m_old = m_ref[st(hh)]
                m_new = jnp.maximum(m_old, x_buf[c][hh])
                m_ref[st(hh)] = m_new
                a_buf[c][hh] = jnp.exp2(m_old - m_new)
                p_buf[c][hh, 0:rows] = jnp.exp2(s_buf[c][hh, 0:rows] - m_new).astype(BF)

        def stage_pv(c, n, p_buf=p_buf, a_buf=a_buf, st=st):
            rows = lc if n is None else tk
            for hh in heads:
                v = vc_ref[0, kv_of(hh), 0] if n is None else vl_ref[0, kv_of(hh), n]
                pv = jnp.dot(v, p_buf[c][hh, 0:rows], preferred_element_type=F32)
                acc_ref[st(hh)] = a_buf[c][hh] * acc_ref[st(hh)] + pv

        for hh in heads:
            m_ref[st(hh)] = jnp.full(m_ref.shape[1:], -1e30, F32)
            acc_ref[st(hh)] = jnp.zeros(acc_ref.shape[1:], F32)
        stage_s(0, None)
        if n_lat:
            stage_s(1, 0)
            stage_sm(0, lc)
            stage_s(0, 1)
            stage_sm(1, tk)
            stage_pv(0, None)

            def pair(it, carry, stage_s=stage_s, stage_sm=stage_sm, stage_pv=stage_pv):
                j = 1 + 2 * it
                stage_s(1, j + 1)
                stage_sm(0, tk)
                stage_pv(1, j - 1)
                stage_s(0, j + 2)
                stage_sm(1, tk)
                stage_pv(0, j)
                return carry

            lax.fori_loop(0, (n_lat - 2) // 2, pair, 0)
            stage_sm(0, tk)
            stage_pv(1, n_lat - 2)
            stage_pv(0, n_lat - 1)
        else:
            stage_sm(0, lc)
            stage_pv(0, None)
        outs = [acc_ref[st(hh), 0:V_DIM] / acc_ref[st(hh), V_DIM:V_DIM + 1] for hh in heads]
        o_ref[0, cols, :] = jnp.concatenate(outs, axis=0).T.astype(BF)


def _attn_call(qt, kc, vct, kl=None, vlt=None):
    b, nh, dq, t = qt.shape
    hkv = kc.shape[1]
    kvps = HEADS_PER_STEP * hkv // nh
    tq = 256
    tstep = min(t, ATTN_TILES_PER_STEP * tq)
    n_tiles = tstep // tq
    n_lat = 0 if kl is None else kl.shape[2]
    assert n_lat % 2 == 0, "the skewed key-block loop handles latent blocks in pairs"
    rows_max = kc.shape[3] if kl is None else max(kc.shape[3], kl.shape[3])
    kvspec = lambda a: pl.BlockSpec((1, kvps) + a.shape[2:], lambda bi, j, i: (bi, j, 0, 0, 0))
    in_specs = [pl.BlockSpec((1, HEADS_PER_STEP, dq, tstep), lambda bi, j, i: (bi, j, 0, i)), kvspec(kc), kvspec(vct)]
    args = [qt, kc, vct]
    if n_lat:
        in_specs += [kvspec(kl), kvspec(vlt)]
        args += [kl, vlt]
    buffer_set = ([pltpu.VMEM((HEADS_PER_STEP, rows_max, tq), F32)] * 2
                  + [pltpu.VMEM((HEADS_PER_STEP, rows_max, tq), BF)] * 2
                  + [pltpu.VMEM((HEADS_PER_STEP, 1, tq), F32)] * 4)
    return pl.pallas_call(
        functools.partial(_attn_kernel, kvps, n_lat, tq),
        out_shape=jax.ShapeDtypeStruct((b, t, nh * V_DIM), BF),
        grid=(b, nh // HEADS_PER_STEP, t // tstep), in_specs=in_specs,
        out_specs=pl.BlockSpec((1, tstep, HEADS_PER_STEP * V_DIM), lambda bi, j, i: (bi, i, j)),
        scratch_shapes=[pltpu.VMEM((n_tiles * HEADS_PER_STEP, 1, tq), F32),
                        pltpu.VMEM((n_tiles * HEADS_PER_STEP, V_ROWS, tq), F32)] + buffer_set * min(n_tiles, 2),
        compiler_params=_cparams(3), name="attention",
    )(*args)


def _rope_tables(seq, rot_dim):
    quarter = rot_dim // 4
    pos = jnp.arange(seq)
    row = (pos // GRID_W).astype(F32)
    col = (pos % GRID_W).astype(F32)
    inv = ROPE_THETA ** (-jnp.arange(quarter, dtype=F32) / quarter)
    ang_r = inv[:, None] * row[None, :]
    ang_c = inv[:, None] * col[None, :]
    cos = jnp.concatenate([jnp.cos(ang_r)] * 2 + [jnp.cos(ang_c)] * 2, axis=0)
    sin = jnp.concatenate([-jnp.sin(ang_r), jnp.sin(ang_r), -jnp.sin(ang_c), jnp.sin(ang_c)], axis=0)
    return cos, sin


def _col(v):
    return v.reshape(-1, 1)


def _row(v):
    return v.reshape(1, -1)


def kernel(x, c, ctx, c_ctx, norm1_g, norm2_g, w_mod, b_mod, w_ff1, w_ff2, conv_w_pw1, conv_w_dw, conv_b_dw, conv_ln_g, conv_ln_b, conv_w_pw2, gqa_w_qkv, gqa_q_norm, gqa_k_norm, gqa_w_o, pool_w, pool_scale, mla_w_dq, mla_q_lora_norm, mla_w_uq, mla_w_dkv, mla_kv_lora_norm, mla_w_ukv, mla_q_norm, mla_k_norm, mla_w_o):
    batch, seq, d = x.shape
    depth = w_mod.shape[0]
    n_ctx = ctx.shape[1]

    cvec = jnp.concatenate([c, c_ctx[None], jnp.zeros((8 - batch - 1, d), F32)], axis=0)
    mod = _mod_call(cvec, w_mod, b_mod).reshape(depth, 8, 6, d)
    mod = jnp.pad(mod, ((0, 0), (0, 0), (0, 2), (0, 0)))
    mod_l = mod[:, :batch]
    mod_c = jnp.broadcast_to(mod[:, batch:batch + 1], (depth, batch, 8, d))

    tab_gqa = _rope_tables(seq, GQA_HEAD_DIM)
    tab_mla = _rope_tables(seq, MLA_ROPE)

    def split_kv(k, vt):
        b_, h_, l_, dq_ = k.shape
        n_ = vt.shape[2]
        return k.reshape(b_, h_, n_, l_ // n_, dq_), vt

    w1_all = w_ff1.astype(BF)
    w2_all = w_ff2.astype(BF)
    xl, xc = x, ctx
    for i in range(depth):
        m = i % N_MIXERS
        j = i // N_MIXERS
        with_ctx = i < depth - 1
        g1 = _row(norm1_g[i])
        g2 = _row(norm2_g[i])
        ml, mc = mod_l[i], mod_c[i]
        oscale = None
        if m == 0:
            wp1 = conv_w_pw1[j].astype(BF)
            wo = conv_w_pw2[j].astype(BF)
            cargs = (jnp.pad(conv_w_dw[j], ((0, 1), (0, 0))), _row(conv_b_dw[j]), _row(conv_ln_g[j]), _row(conv_ln_b[j]))
            al = _dwconv_call(_glu_call(xl, ml, g1, wp1), *cargs)
            ac = _dwconv_call(_glu_call(xc, mc, g1, wp1), *cargs) if with_ctx else None
        elif m == 1:
            wt = gqa_w_qkv[j].T.astype(BF)
            wo = gqa_w_o[j].astype(BF)
            qg, kg = _col(gqa_q_norm[j]), _col(gqa_k_norm[j])
            ql, kl, vl = _gqa_proj_call(xl, ml, g1, wt, qg, kg, tab_gqa)
            qc, kc, vc = _gqa_proj_call(xc, mc, g1, wt, qg, kg, None)
            kc5, vc5 = split_kv(kc, vc)
            al = _attn_call(ql, kc5, vc5, *split_kv(kl, vl))
            ac = _attn_call(qc, kc5, vc5) if with_ctx else None
        elif m == 2:
            eye = jnp.eye(len(POOL_WINDOWS), dtype=F32)
            wo = jnp.einsum('gcd,gh->gchd', pool_w[j], eye).reshape(d, d).astype(BF)
            oscale = _row(pool_scale[j])
            al = _pool_call(xl, ml, g1)
            ac = _pool_call(xc, mc, g1) if with_ctx else None
        else:
            wo = mla_w_o[j].astype(BF)
            pargs = (mla_w_dq[j].T.astype(BF), _col(mla_q_lora_norm[j]), mla_w_uq[j].T.astype(BF),
                     mla_w_dkv[j].T.astype(BF), _col(mla_kv_lora_norm[j]), mla_w_ukv[j].T.astype(BF),
                     _col(mla_q_norm[j]), _col(mla_k_norm[j]))
            ql, kl, vl = _mla_proj_call(xl, ml, g1, *pargs, tab_mla, True)
            pc = _mla_proj_call(xc, mc, g1, *pargs, None, with_ctx)
            kc5, vc5 = split_kv(*pc[-2:])
            al = _attn_call(ql, kc5, vc5, *split_kv(kl, vl))
            ac = _attn_call(pc[0], kc5, vc5) if with_ctx else None
        xl = _post_call(xl, al, ml, g2, wo, w1_all, w2_all, i, oscale)
        if with_ctx:
            xc = _post_call(xc, ac, mc, g2, wo, w1_all, w2_all, i, oscale)
    return xl
```

```python
import functools

import jax
import jax.numpy as jnp
from jax import lax
from jax.experimental import pallas as pl
from jax.experimental.pallas import tpu as pltpu

D_MODEL = 1024
GRID_W = 64
ROPE_THETA = 10000.0
NORM_EPS = 1e-6
CONV_WIDTH = 31
CONV_PAD = CONV_WIDTH // 2
GQA_HEADS = 16
GQA_KV_HEADS = 4
GQA_HEAD_DIM = 64
POOL_WINDOWS = (2, 4, 8, 16)
POOL_GROUP = 256
MLA_HEADS = 16
MLA_NOPE = 64
MLA_ROPE = 32
MLA_V = 64
MLA_QK = MLA_NOPE + MLA_ROPE
MLA_Q_LORA = 768
MLA_KV_LORA = 256
D_FF = 4 * D_MODEL
N_MIXERS = 4

HALO = 16
HEADS_PER_STEP = 4
V_DIM = 64
PROJ_ROW_CHUNK = 512
ATTN_TQ = 256
ATTN_TILES_PER_STEP = 4
ATTN_TILE_GROUP = 2
V_ROWS = V_DIM + 16
VMEM_LIMIT = 52 * 1024 * 1024
LOG2E = 1.4426950408889634
BF = jnp.bfloat16
F32 = jnp.float32


def _cparams(n_axes):
    return pltpu.CompilerParams(dimension_semantics=("arbitrary",) * n_axes, vmem_limit_bytes=VMEM_LIMIT)


def _resident(shape):
    nd = len(shape)
    return pl.BlockSpec(shape, lambda *_: (0,) * nd, pipeline_mode=pl.Buffered(1))


def _norm_mod(x, g, shift, scale):
    ms = jnp.mean(x * x, axis=-1, keepdims=True)
    y = x * lax.rsqrt(ms + NORM_EPS) * g
    return y * (1.0 + scale) + shift


def _rms_rows(xt, gain):
    ms = jnp.mean(xt * xt, axis=0, keepdims=True)
    return xt * lax.rsqrt(ms + NORM_EPS) * gain


def _rope_rows(xt, cos, sin, quarter):
    q = quarter
    rot = jnp.concatenate([xt[q:2 * q], xt[0:q], xt[3 * q:4 * q], xt[2 * q:3 * q]], axis=0)
    return xt * cos + rot * sin


def _mod_kernel(c_ref, w_ref, b_ref, o_ref):
    c = c_ref[...]
    s = (c * jax.nn.sigmoid(c)).astype(BF)
    o_ref[0] = jnp.dot(s, w_ref[0].astype(BF), preferred_element_type=F32) + b_ref[0]


def _mod_call(cvec, w_mod, b_mod):
    depth, d, n = w_mod.shape
    rows = cvec.shape[0]
    tn = 1536
    return pl.pallas_call(
        _mod_kernel,
        out_shape=jax.ShapeDtypeStruct((depth, rows, n), F32),
        grid=(depth, n // tn),
        in_specs=[pl.BlockSpec((rows, d), lambda l, j: (0, 0)),
                  pl.BlockSpec((1, d, tn), lambda l, j: (l, 0, j)),
                  pl.BlockSpec((1, 1, tn), lambda l, j: (l, 0, j))],
        out_specs=pl.BlockSpec((1, rows, tn), lambda l, j: (l, 0, j)),
        compiler_params=_cparams(2), name="adaln_mod",
    )(cvec, w_mod, b_mod.reshape(depth, 1, n))


def _post_kernel(has_oscale, x_ref, a_ref, mod_ref, g2_ref, wo_ref, w1_ref, w2_ref, *rest):
    if has_oscale:
        os_ref, o_ref = rest
    else:
        (o_ref,) = rest
    mod = mod_ref[0]
    y = jnp.dot(a_ref[0], wo_ref[...], preferred_element_type=F32)
    if has_oscale:
        y = y * os_ref[...]
    x1 = x_ref[0] + mod[2:3] * y
    h = _norm_mod(x1, g2_ref[...], mod[3:4], mod[4:5]).astype(BF)
    ffc = D_MODEL
    acc = jnp.zeros_like(x1)
    for c in range(D_FF // ffc):
        t = jnp.dot(h, w1_ref[0, :, c * ffc:(c + 1) * ffc], preferred_element_type=F32)
        t = jnp.maximum(t, 0.0)
        acc = acc + jnp.dot((t * t).astype(BF), w2_ref[0, c * ffc:(c + 1) * ffc, :], preferred_element_type=F32)
    o_ref[0] = x1 + mod[5:6] * acc


def _post_call(x, a, mod, g2, wo, w1, w2, layer, oscale=None):
    b, t, d = x.shape
    tm = min(t, 512)
    tile = lambda: pl.BlockSpec((1, tm, d), lambda bi, i: (bi, i, 0))
    layer_w = lambda w: pl.BlockSpec((1,) + w.shape[1:], lambda bi, i: (layer, 0, 0), pipeline_mode=pl.Buffered(1))
    in_specs = [tile(), tile(), pl.BlockSpec((1, 8, d), lambda bi, i: (bi, 0, 0)),
                _resident((1, d)), _resident(wo.shape), layer_w(w1), layer_w(w2)]
    args = [x, a, mod, g2, wo, w1, w2]
    if oscale is not None:
        in_specs.append(_resident((1, d)))
        args.append(oscale)
    return pl.pallas_call(
        functools.partial(_post_kernel, oscale is not None),
        out_shape=jax.ShapeDtypeStruct(x.shape, F32),
        grid=(b, t // tm), in_specs=in_specs, out_specs=tile(),
        compiler_params=_cparams(2), name="outproj_mlp",
    )(*args)


def _glu_kernel(x_ref, mod_ref, g_ref, w_ref, o_ref):
    mod = mod_ref[0]
    h = _norm_mod(x_ref[0], g_ref[...], mod[0:1], mod[1:2]).astype(BF)
    u = jnp.dot(h, w_ref[...], preferred_element_type=F32)
    o_ref[0] = u[:, :D_MODEL] * jax.nn.sigmoid(u[:, D_MODEL:])


def _glu_call(x, mod, g, w):
    b, t, d = x.shape
    tm = min(t, 512)
    tile = lambda: pl.BlockSpec((1, tm, d), lambda bi, i: (bi, i, 0))
    return pl.pallas_call(
        _glu_kernel, out_shape=jax.ShapeDtypeStruct(x.shape, F32), grid=(b, t // tm),
        in_specs=[tile(), pl.BlockSpec((1, 8, d), lambda bi, i: (bi, 0, 0)), _resident((1, d)), _resident(w.shape)],
        out_specs=tile(), compiler_params=_cparams(2), name="conv_pw1_glu",
    )(x, mod, g, w)


def _halo_specs(t, tm, d):
    r = tm // HALO
    last = t // HALO - 1
    return [pl.BlockSpec((1, tm, d), lambda bi, i: (bi, i, 0)),
            pl.BlockSpec((1, HALO, d), lambda bi, i: (bi, jnp.maximum(i * r - 1, 0), 0)),
            pl.BlockSpec((1, HALO, d), lambda bi, i: (bi, jnp.minimum((i + 1) * r, last), 0))]


def _dwconv_kernel(tm, u_ref, up_ref, un_ref, w_ref, b_ref, lg_ref, lb_ref, o_ref, win_ref, y_ref):
    i = pl.program_id(1)
    n = pl.num_programs(1)
    win_ref[0:HALO] = jnp.where(i > 0, up_ref[0], 0.0)
    win_ref[HALO:HALO + tm] = u_ref[0]
    win_ref[HALO + tm:] = jnp.where(i < n - 1, un_ref[0], 0.0)
    rc, cc = 128, 128
    span = rc + 2 * HALO
    for c0 in range(0, D_MODEL, cc):
        for r0 in range(0, tm, rc):
            x = win_ref[r0:r0 + span, c0:c0 + cc]
            acc = jnp.broadcast_to(b_ref[:, c0:c0 + cc], (rc, cc))
            for sub in range(8):
                xs = x if sub == 0 else pltpu.roll(x, span - sub, axis=0)
                for off in range(sub, 2 * HALO, 8):
                    j = off - (HALO - CONV_PAD)
                    if 0 <= j < CONV_WIDTH:
                        acc = acc + xs[off - sub:off - sub + rc] * w_ref[pl.ds(j, 1), c0:c0 + cc]
            y_ref[r0:r0 + rc, c0:c0 + cc] = acc
    y = y_ref[...]
    mu = jnp.mean(y, axis=-1, keepdims=True)
    yc = y - mu
    var = jnp.mean(yc * yc, axis=-1, keepdims=True)
    z = yc * lax.rsqrt(var + NORM_EPS) * lg_ref[...] + lb_ref[...]
    o_ref[0] = (z * jax.nn.sigmoid(z)).astype(BF)


def _dwconv_call(u, w_dw, b_dw, ln_g, ln_b):
    b, t, d = u.shape
    tm = 128
    return pl.pallas_call(
        functools.partial(_dwconv_kernel, tm),
        out_shape=jax.ShapeDtypeStruct(u.shape, BF), grid=(b, t // tm),
        in_specs=_halo_specs(t, tm, d) + [_resident(w_dw.shape), _resident((1, d)), _resident((1, d)), _resident((1, d))],
        out_specs=pl.BlockSpec((1, tm, d), lambda bi, i: (bi, i, 0)),
        scratch_shapes=[pltpu.VMEM((tm + 2 * HALO, d), F32), pltpu.VMEM((tm, d), F32)],
        compiler_params=_cparams(2), name="dwconv_ln_silu",
    )(u, u, u, w_dw, b_dw, ln_g, ln_b)


def _pool_kernel(tm, t_total, x_ref, xp_ref, xn_ref, mod_ref, g_ref, o_ref, win_ref):
    i = pl.program_id(1)
    n = pl.num_programs(1)
    mod = mod_ref[0]
    nm = lambda v: _norm_mod(v, g_ref[...], mod[0:1], mod[1:2])
    win_ref[0:HALO] = jnp.where(i > 0, nm(xp_ref[0]), 0.0)
    win_ref[HALO:HALO + tm] = nm(x_ref[0])
    win_ref[HALO + tm:] = jnp.where(i < n - 1, nm(xn_ref[0]), 0.0)
    pos = i * tm + lax.broadcasted_iota(jnp.int32, (tm, 1), 0)
    for gi, w in enumerate(POOL_WINDOWS):
        c0 = gi * POOL_GROUP
        half = w // 2
        acc = win_ref[pl.ds(HALO - half, tm), c0:c0 + POOL_GROUP]
        for o in range(-half + 1, w - half):
            acc = acc + win_ref[pl.ds(HALO + o, tm), c0:c0 + POOL_GROUP]
        lo = jnp.maximum(pos - half, 0)
        hi = jnp.minimum(pos + (w - half), t_total)
        cnt = (hi - lo).astype(F32)
        o_ref[0, :, c0:c0 + POOL_GROUP] = (acc / cnt - win_ref[HALO:HALO + tm, c0:c0 + POOL_GROUP]).astype(BF)


def _pool_call(x, mod, g):
    b, t, d = x.shape
    tm = 256
    return pl.pallas_call(
        functools.partial(_pool_kernel, tm, t),
        out_shape=jax.ShapeDtypeStruct(x.shape, BF), grid=(b, t // tm),
        in_specs=_halo_specs(t, tm, d) + [pl.BlockSpec((1, 8, d), lambda bi, i: (bi, 0, 0)), _resident((1, d))],
        out_specs=pl.BlockSpec((1, tm, d), lambda bi, i: (bi, i, 0)),
        scratch_shapes=[pltpu.VMEM((tm + 2 * HALO, d), F32)],
        compiler_params=_cparams(2), name="pool_windows",
    )(x, x, x, mod, g)


def _gqa_proj_kernel(rope, x_ref, mod_ref, g_ref, wt_ref, qg_ref, kg_ref, *rest):
    if rope:
        cos_ref, sin_ref, q_ref, k_ref, v_ref, y_ref = rest
    else:
        q_ref, k_ref, v_ref, y_ref = rest
    mod = mod_ref[0]
    h = _norm_mod(x_ref[0], g_ref[...], mod[0:1], mod[1:2]).astype(BF)
    for r0 in range(0, y_ref.shape[0], PROJ_ROW_CHUNK):
        y_ref[r0:r0 + PROJ_ROW_CHUNK] = lax.dot_general(wt_ref[r0:r0 + PROJ_ROW_CHUNK], h, (((1,), (1,)), ((), ())),
                                                        preferred_element_type=F32)
    hd = GQA_HEAD_DIM
    nq = GQA_HEADS * hd
    nk = GQA_KV_HEADS * hd

    def head(row0, gain_ref):
        z = _rms_rows(y_ref[row0:row0 + hd], gain_ref[...])
        if rope:
            z = _rope_rows(z, cos_ref[...], sin_ref[...], hd // 4)
        return z

    for hh in range(GQA_HEADS):
        q_ref[0, hh] = (head(hh * hd, qg_ref) * (hd ** -0.5 * LOG2E)).astype(BF)
    for hh in range(GQA_KV_HEADS):
        k_ref[0, hh] = head(nq + hh * hd, kg_ref).T.astype(BF)
        v_ref[0, hh, 0, 0:V_DIM] = y_ref[nq + nk + hh * hd:nq + nk + (hh + 1) * hd].astype(BF)
        v_ref[0, hh, 0, V_DIM:V_ROWS] = jnp.ones((V_ROWS - V_DIM, v_ref.shape[-1]), BF)


def _gqa_proj_call(x, mod, g, wt, qg, kg, tables):
    b, t, d = x.shape
    tm = min(t, 512)
    nt = t // tm
    hd = GQA_HEAD_DIM
    rope = tables is not None
    in_specs = [pl.BlockSpec((1, tm, d), lambda bi, i: (bi, i, 0)), pl.BlockSpec((1, 8, d), lambda bi, i: (bi, 0, 0)),
                _resident((1, d)), _resident(wt.shape), _resident((hd, 1)), _resident((hd, 1))]
    args = [x, mod, g, wt, qg, kg]
    if rope:
        in_specs += [pl.BlockSpec((hd, tm), lambda bi, i: (0, i))] * 2
        args += list(tables)
    return pl.pallas_call(
        functools.partial(_gqa_proj_kernel, rope),
        out_shape=(jax.ShapeDtypeStruct((b, GQA_HEADS, hd, t), BF),
                   jax.ShapeDtypeStruct((b, GQA_KV_HEADS, t, hd), BF),
                   jax.ShapeDtypeStruct((b, GQA_KV_HEADS, nt, V_ROWS, tm), BF)),
        grid=(b, nt), in_specs=in_specs,
        out_specs=(pl.BlockSpec((1, GQA_HEADS, hd, tm), lambda bi, i: (bi, 0, 0, i)),
                   pl.BlockSpec((1, GQA_KV_HEADS, tm, hd), lambda bi, i: (bi, 0, i, 0)),
                   pl.BlockSpec((1, GQA_KV_HEADS, 1, V_ROWS, tm), lambda bi, i: (bi, 0, i, 0, 0))),
        scratch_shapes=[pltpu.VMEM((wt.shape[0], tm), F32)],
        compiler_params=_cparams(2), name="gqa_qkv_proj",
    )(*args)


def _mla_proj_kernel(rope, need_q, x_ref, mod_ref, g_ref, wdq_ref, qln_ref, wuq_ref, wdkv_ref, kvln_ref, wukv_ref,
                     qg_ref, kg_ref, *rest):
    rest = list(rest)
    if rope:
        cos_ref, sin_ref = rest[:2]
        rest = rest[2:]
    if need_q:
        q_ref, k_ref, v_ref, yq_ref, ykv_ref = rest
    else:
        k_ref, v_ref, ykv_ref = rest
    mod = mod_ref[0]
    nt_dims = (((1,), (1,)), ((), ()))
    h = _norm_mod(x_ref[0], g_ref[...], mod[0:1], mod[1:2]).astype(BF)

    def tail_rope(z):
        if not rope:
            return z
        return jnp.concatenate([z[:MLA_NOPE], _rope_rows(z[MLA_NOPE:], cos_ref[...], sin_ref[...], MLA_ROPE // 4)], axis=0)

    if need_q:
        cq = lax.dot_general(wdq_ref[...], h, nt_dims, preferred_element_type=F32)
        cq = _rms_rows(cq, qln_ref[...]).astype(BF)
        for r0 in range(0, yq_ref.shape[0], PROJ_ROW_CHUNK):
            yq_ref[r0:r0 + PROJ_ROW_CHUNK] = jnp.dot(wuq_ref[r0:r0 + PROJ_ROW_CHUNK], cq, preferred_element_type=F32)
        for hh in range(MLA_HEADS):
            z = tail_rope(_rms_rows(yq_ref[hh * MLA_QK:(hh + 1) * MLA_QK], qg_ref[...]))
            q_ref[0, hh] = (z * (MLA_QK ** -0.5 * LOG2E)).astype(BF)
    dkv = lax.dot_general(wdkv_ref[...], h, nt_dims, preferred_element_type=F32)
    ckv = _rms_rows(dkv[:MLA_KV_LORA], kvln_ref[...]).astype(BF)
    k_rope = dkv[MLA_KV_LORA:]
    for r0 in range(0, ykv_ref.shape[0], PROJ_ROW_CHUNK):
        ykv_ref[r0:r0 + PROJ_ROW_CHUNK] = jnp.dot(wukv_ref[r0:r0 + PROJ_ROW_CHUNK], ckv, preferred_element_type=F32)
    per = MLA_NOPE + MLA_V
    for hh in range(MLA_HEADS):
        kf = jnp.concatenate([ykv_ref[hh * per:hh * per + MLA_NOPE], k_rope], axis=0)
        k_ref[0, hh] = tail_rope(_rms_rows(kf, kg_ref[...])).T.astype(BF)
        v_ref[0, hh, 0, 0:V_DIM] = ykv_ref[hh * per + MLA_NOPE:(hh + 1) * per].astype(BF)
        v_ref[0, hh, 0, V_DIM:V_ROWS] = jnp.ones((V_ROWS - V_DIM, v_ref.shape[-1]), BF)


def _mla_proj_call(x, mod, g, wdq_t, qln, wuq_t, wdkv_t, kvln, wukv_t, qg, kg, tables, need_q):
    b, t, d = x.shape
    tm = min(t, 512)
    nt = t // tm
    rope = tables is not None
    in_specs = [pl.BlockSpec((1, tm, d), lambda bi, i: (bi, i, 0)), pl.BlockSpec((1, 8, d), lambda bi, i: (bi, 0, 0)),
                _resident((1, d)), _resident(wdq_t.shape), _resident(qln.shape), _resident(wuq_t.shape),
                _resident(wdkv_t.shape), _resident(kvln.shape), _resident(wukv_t.shape),
                _resident(qg.shape), _resident(kg.shape)]
    args = [x, mod, g, wdq_t, qln, wuq_t, wdkv_t, kvln, wukv_t, qg, kg]
    if rope:
        in_specs += [pl.BlockSpec((MLA_ROPE, tm), lambda bi, i: (0, i))] * 2
        args += list(tables)
    out_shape = [jax.ShapeDtypeStruct((b, MLA_HEADS, t, MLA_QK), BF),
                 jax.ShapeDtypeStruct((b, MLA_HEADS, nt, V_ROWS, tm), BF)]
    out_specs = [pl.BlockSpec((1, MLA_HEADS, tm, MLA_QK), lambda bi, i: (bi, 0, i, 0)),
                 pl.BlockSpec((1, MLA_HEADS, 1, V_ROWS, tm), lambda bi, i: (bi, 0, i, 0, 0))]
    scratch = [pltpu.VMEM((wukv_t.shape[0], tm), F32)]
    if need_q:
        out_shape.insert(0, jax.ShapeDtypeStruct((b, MLA_HEADS, MLA_QK, t), BF))
        out_specs.insert(0, pl.BlockSpec((1, MLA_HEADS, MLA_QK, tm), lambda bi, i: (bi, 0, 0, i)))
        scratch.insert(0, pltpu.VMEM((wuq_t.shape[0], tm), F32))
    return pl.pallas_call(
        functools.partial(_mla_proj_kernel, rope, need_q),
        out_shape=tuple(out_shape), grid=(b, nt), in_specs=in_specs, out_specs=tuple(out_specs),
        scratch_shapes=scratch, compiler_params=_cparams(2), name="mla_proj",
    )(*args)


def _attn_kernel(kv_per_step, n_lat, tq, group, q_ref, kc_ref, vc_ref, *rest):
    if n_lat:
        kl_ref, vl_ref, o_ref, m_ref, acc_ref, *bufs = rest
    else:
        o_ref, m_ref, acc_ref, *bufs = rest
    kv_of = lambda hh: hh * kv_per_step // HEADS_PER_STEP
    lc = kc_ref.shape[3]
    tk = kl_ref.shape[3] if n_lat else 0
    n_groups = q_ref.shape[3] // (tq * group)
    per_group = group * HEADS_PER_STEP

    for gi in range(n_groups):
        s0, s1, p0, p1, a0, a1, x0, x1 = bufs[8 * (gi % 2):8 * (gi % 2 + 1)]
        s_buf, p_buf, a_buf, x_buf = (s0, s1), (p0, p1), (a0, a1), (x0, x1)
        streams = [(slice((gi * group + t) * tq, (gi * group + t + 1) * tq), hh, gi * per_group + t * HEADS_PER_STEP + hh)
                   for t in range(group) for hh in range(HEADS_PER_STEP)]

        def stage_s(c, n):
            rows = lc if n is None else tk
            for e, (cols, hh, _) in enumerate(streams):
                k = kc_ref[0, kv_of(hh), 0] if n is None else kl_ref[0, kv_of(hh), n]
                s = jnp.dot(k, q_ref[0, hh, :, cols], preferred_element_type=F32)
                s_buf[c][e, 0:rows] = s
                x_buf[c][e] = jnp.max(s, axis=0, keepdims=True)

        def stage_sm(c, rows):
            for e, (_, _, st) in enumerate(streams):
                m_old = m_ref[st]
                m_new = jnp.maximum(m_old, x_buf[c][e])
                m_ref[st] = m_new
                a_buf[c][e] = jnp.exp2(m_old - m_new)
                p_buf[c][e, 0:rows] = jnp.exp2(s_buf[c][e, 0:rows] - m_new).astype(BF)

        def stage_pv(c, n):
            rows = lc if n is None else tk
            for e, (_, hh, st) in enumerate(streams):
                v = vc_ref[0, kv_of(hh), 0] if n is None else vl_ref[0, kv_of(hh), n]
                pv = jnp.dot(v, p_buf[c][e, 0:rows], preferred_element_type=F32)
                acc_ref[st] = a_buf[c][e] * acc_ref[st] + pv

        for _, _, st in streams:
            m_ref[st] = jnp.full(m_ref.shape[1:], -1e30, F32)
            acc_ref[st] = jnp.zeros(acc_ref.shape[1:], F32)
        stage_s(0, None)
        if n_lat:
            stage_s(1, 0)
            stage_sm(0, lc)
            stage_s(0, 1)
            stage_sm(1, tk)
            stage_pv(0, None)
            for j in range(1, n_lat - 1, 2):
                stage_s(1, j + 1)
                stage_sm(0, tk)
                stage_pv(1, j - 1)
                stage_s(0, j + 2)
                stage_sm(1, tk)
                stage_pv(0, j)
            stage_sm(0, tk)
            stage_pv(1, n_lat - 2)
            stage_pv(0, n_lat - 1)
        else:
            stage_sm(0, lc)
            stage_pv(0, None)
        for t in range(group):
            tile = streams[t * HEADS_PER_STEP:(t + 1) * HEADS_PER_STEP]
            outs = [acc_ref[st, 0:V_DIM] / acc_ref[st, V_DIM:V_DIM + 1] for _, _, st in tile]
            o_ref[0, tile[0][0], :] = jnp.concatenate(outs, axis=0).T.astype(BF)


def _attn_call(qt, kc, vct, kl=None, vlt=None):
    b, nh, dq, t = qt.shape
    hkv = kc.shape[1]
    kvps = HEADS_PER_STEP * hkv // nh
    tq = ATTN_TQ
    tstep = min(t, ATTN_TILES_PER_STEP * tq)
    n_tiles = tstep // tq
    group = min(ATTN_TILE_GROUP, n_tiles)
    n_lat = 0 if kl is None else kl.shape[2]
    assert n_lat % 2 == 0, "the skewed key-block schedule handles latent blocks in pairs"
    rows_max = kc.shape[3] if kl is None else max(kc.shape[3], kl.shape[3])
    kvspec = lambda a: pl.BlockSpec((1, kvps) + a.shape[2:], lambda bi, j, i: (bi, j, 0, 0, 0))
    in_specs = [pl.BlockSpec((1, HEADS_PER_STEP, dq, tstep), lambda bi, j, i: (bi, j, 0, i)), kvspec(kc), kvspec(vct)]
    args = [qt, kc, vct]
    if n_lat:
        in_specs += [kvspec(kl), kvspec(vlt)]
        args += [kl, vlt]
    streams = group * HEADS_PER_STEP
    buffer_set = ([pltpu.VMEM((streams, rows_max, tq), F32)] * 2 + [pltpu.VMEM((streams, rows_max, tq), BF)] * 2
                  + [pltpu.VMEM((streams, 1, tq), F32)] * 4)
    return pl.pallas_call(
        functools.partial(_attn_kernel, kvps, n_lat, tq, group),
        out_shape=jax.ShapeDtypeStruct((b, t, nh * V_DIM), BF),
        grid=(b, nh // HEADS_PER_STEP, t // tstep), in_specs=in_specs,
        out_specs=pl.BlockSpec((1, tstep, HEADS_PER_STEP * V_DIM), lambda bi, j, i: (bi, i, j)),
        scratch_shapes=[pltpu.VMEM((n_tiles * HEADS_PER_STEP, 1, tq), F32),
                        pltpu.VMEM((n_tiles * HEADS_PER_STEP, V_ROWS, tq), F32)]
        + buffer_set * min(n_tiles // group, 2),
        compiler_params=_cparams(3), name="attention",
    )(*args)


def _rope_tables(seq, rot_dim):
    quarter = rot_dim // 4
    pos = jnp.arange(seq)
    row = (pos // GRID_W).astype(F32)
    col = (pos % GRID_W).astype(F32)
    inv = ROPE_THETA ** (-jnp.arange(quarter, dtype=F32) / quarter)
    ang_r = inv[:, None] * row[None, :]
    ang_c = inv[:, None] * col[None, :]
    cos = jnp.concatenate([jnp.cos(ang_r)] * 2 + [jnp.cos(ang_c)] * 2, axis=0)
    sin = jnp.concatenate([-jnp.sin(ang_r), jnp.sin(ang_r), -jnp.sin(ang_c), jnp.sin(ang_c)], axis=0)
    return cos, sin


def _col(v):
    return v.reshape(-1, 1)


def _row(v):
    return v.reshape(1, -1)


def kernel(x, c, ctx, c_ctx, norm1_g, norm2_g, w_mod, b_mod, w_ff1, w_ff2, conv_w_pw1, conv_w_dw, conv_b_dw, conv_ln_g, conv_ln_b, conv_w_pw2, gqa_w_qkv, gqa_q_norm, gqa_k_norm, gqa_w_o, pool_w, pool_scale, mla_w_dq, mla_q_lora_norm, mla_w_uq, mla_w_dkv, mla_kv_lora_norm, mla_w_ukv, mla_q_norm, mla_k_norm, mla_w_o):
    batch, seq, d = x.shape
    depth = w_mod.shape[0]

    cvec = jnp.concatenate([c, c_ctx[None], jnp.zeros((8 - batch - 1, d), F32)], axis=0)
    mod = _mod_call(cvec, w_mod, b_mod).reshape(depth, 8, 6, d)
    mod = jnp.pad(mod, ((0, 0), (0, 0), (0, 2), (0, 0)))
    mod_l = mod[:, :batch]
    mod_c = jnp.broadcast_to(mod[:, batch:batch + 1], (depth, batch, 8, d))

    tab_gqa = _rope_tables(seq, GQA_HEAD_DIM)
    tab_mla = _rope_tables(seq, MLA_ROPE)

    def split_kv(k, vt):
        b_, h_, l_, dq_ = k.shape
        n_ = vt.shape[2]
        return k.reshape(b_, h_, n_, l_ // n_, dq_), vt

    w1_all = w_ff1.astype(BF)
    w2_all = w_ff2.astype(BF)
    xl, xc = x, ctx
    for i in range(depth):
        m = i % N_MIXERS
        j = i // N_MIXERS
        with_ctx = i < depth - 1
        g1 = _row(norm1_g[i])
        g2 = _row(norm2_g[i])
        ml, mc = mod_l[i], mod_c[i]
        oscale = None
        if m == 0:
            wp1 = conv_w_pw1[j].astype(BF)
            wo = conv_w_pw2[j].astype(BF)
            cargs = (jnp.pad(conv_w_dw[j], ((0, 1), (0, 0))), _row(conv_b_dw[j]), _row(conv_ln_g[j]), _row(conv_ln_b[j]))
            al = _dwconv_call(_glu_call(xl, ml, g1, wp1), *cargs)
            ac = _dwconv_call(_glu_call(xc, mc, g1, wp1), *cargs) if with_ctx else None
        elif m == 1:
            wt = gqa_w_qkv[j].T.astype(BF)
            wo = gqa_w_o[j].astype(BF)
            qg, kg = _col(gqa_q_norm[j]), _col(gqa_k_norm[j])
            ql, kl, vl = _gqa_proj_call(xl, ml, g1, wt, qg, kg, tab_gqa)
            qc, kc, vc = _gqa_proj_call(xc, mc, g1, wt, qg, kg, None)
            kc5, vc5 = split_kv(kc, vc)
            al = _attn_call(ql, kc5, vc5, *split_kv(kl, vl))
            ac = _attn_call(qc, kc5, vc5) if with_ctx else None
        elif m == 2:
            eye = jnp.eye(len(POOL_WINDOWS), dtype=F32)
            wo = jnp.einsum('gcd,gh->gchd', pool_w[j], eye).reshape(d, d).astype(BF)
            oscale = _row(pool_scale[j])
            al = _pool_call(xl, ml, g1)
            ac = _pool_call(xc, mc, g1) if with_ctx else None
        else:
            wo = mla_w_o[j].astype(BF)
            pargs = (mla_w_dq[j].T.astype(BF), _col(mla_q_lora_norm[j]), mla_w_uq[j].T.astype(BF),
                     mla_w_dkv[j].T.astype(BF), _col(mla_kv_lora_norm[j]), mla_w_ukv[j].T.astype(BF),
                     _col(mla_q_norm[j]), _col(mla_k_norm[j]))
            ql, kl, vl = _mla_proj_call(xl, ml, g1, *pargs, tab_mla, True)
            pc = _mla_proj_call(xc, mc, g1, *pargs, None, with_ctx)
            kc5, vc5 = split_kv(*pc[-2:])
            al = _attn_call(ql, kc5, vc5, *split_kv(kl, vl))
            ac = _attn_call(pc[0], kc5, vc5) if with_ctx else None
        xl = _post_call(xl, al, ml, g2, wo, w1_all, w2_all, i, oscale)
        if with_ctx:
            xc = _post_call(xc, ac, mc, g2, wo, w1_all, w2_all, i, oscale)
    return xl
```

```python
import functools

import jax
import jax.numpy as jnp
from jax import lax
from jax.experimental import pallas as pl
from jax.experimental.pallas import tpu as pltpu

D_MODEL = 1024
GRID_W = 64
ROPE_THETA = 10000.0
NORM_EPS = 1e-6
CONV_WIDTH = 31
CONV_PAD = CONV_WIDTH // 2
GQA_HEADS = 16
GQA_KV_HEADS = 4
GQA_HEAD_DIM = 64
POOL_WINDOWS = (2, 4, 8, 16)
POOL_GROUP = 256
MLA_HEADS = 16
MLA_NOPE = 64
MLA_ROPE = 32
MLA_V = 64
MLA_QK = MLA_NOPE + MLA_ROPE
MLA_Q_LORA = 768
MLA_KV_LORA = 256
D_FF = 4 * D_MODEL
N_MIXERS = 4

HALO = 16
HEADS_PER_STEP = 4
V_DIM = 64
PROJ_ROW_CHUNK = 512
ATTN_TQ = 256
ATTN_TILES_PER_STEP = 4
ATTN_TILE_GROUP = 2
V_ROWS = V_DIM + 16
VMEM_LIMIT = 52 * 1024 * 1024
LOG2E = 1.4426950408889634
BF = jnp.bfloat16
F32 = jnp.float32


def _cparams(n_axes):
    return pltpu.CompilerParams(dimension_semantics=("arbitrary",) * n_axes, vmem_limit_bytes=VMEM_LIMIT)


def _resident(shape):
    nd = len(shape)
    return pl.BlockSpec(shape, lambda *_: (0,) * nd, pipeline_mode=pl.Buffered(1))


def _norm_mod(x, g, shift, scale):
    ms = jnp.mean(x * x, axis=-1, keepdims=True)
    y = x * lax.rsqrt(ms + NORM_EPS) * g
    return y * (1.0 + scale) + shift


def _rms_rows(xt, gain):
    ms = jnp.mean(xt * xt, axis=0, keepdims=True)
    return xt * lax.rsqrt(ms + NORM_EPS) * gain


def _rope_rows(xt, cos, sin, quarter):
    q = quarter
    rot = jnp.concatenate([xt[q:2 * q], xt[0:q], xt[3 * q:4 * q], xt[2 * q:3 * q]], axis=0)
    return xt * cos + rot * sin


def _mod_kernel(c_ref, w_ref, b_ref, o_ref):
    c = c_ref[...]
    s = (c * jax.nn.sigmoid(c)).astype(BF)
    o_ref[0] = jnp.dot(s, w_ref[0].astype(BF), preferred_element_type=F32) + b_ref[0]


def _mod_call(cvec, w_mod, b_mod):
    depth, d, n = w_mod.shape
    rows = cvec.shape[0]
    tn = 1536
    return pl.pallas_call(
        _mod_kernel,
        out_shape=jax.ShapeDtypeStruct((depth, rows, n), F32),
        grid=(depth, n // tn),
        in_specs=[pl.BlockSpec((rows, d), lambda l, j: (0, 0)),
                  pl.BlockSpec((1, d, tn), lambda l, j: (l, 0, j)),
                  pl.BlockSpec((1, 1, tn), lambda l, j: (l, 0, j))],
        out_specs=pl.BlockSpec((1, rows, tn), lambda l, j: (l, 0, j)),
        compiler_params=_cparams(2), name="adaln_mod",
    )(cvec, w_mod, b_mod.reshape(depth, 1, n))


def _post_kernel(has_oscale, x_ref, a_ref, mod_ref, g2_ref, wo_ref, w1_ref, w2_ref, *rest):
    if has_oscale:
        os_ref, o_ref = rest
    else:
        (o_ref,) = rest
    mod = mod_ref[0]
    y = jnp.dot(a_ref[0], wo_ref[...], preferred_element_type=F32)
    if has_oscale:
        y = y * os_ref[...]
    x1 = x_ref[0] + mod[2:3] * y
    h = _norm_mod(x1, g2_ref[...], mod[3:4], mod[4:5]).astype(BF)
    ffc = D_MODEL
    acc = jnp.zeros_like(x1)
    for c in range(D_FF // ffc):
        t = jnp.dot(h, w1_ref[0, :, c * ffc:(c + 1) * ffc], preferred_element_type=F32)
        t = jnp.maximum(t, 0.0)
        acc = acc + jnp.dot((t * t).astype(BF), w2_ref[0, c * ffc:(c + 1) * ffc, :], preferred_element_type=F32)
    o_ref[0] = x1 + mod[5:6] * acc


def _post_call(x, a, mod, g2, wo, w1, w2, layer, oscale=None):
    b, t, d = x.shape
    tm = min(t, 512)
    tile = lambda: pl.BlockSpec((1, tm, d), lambda bi, i: (bi, i, 0))
    layer_w = lambda w: pl.BlockSpec((1,) + w.shape[1:], lambda bi, i: (layer, 0, 0), pipeline_mode=pl.Buffered(1))
    in_specs = [tile(), tile(), pl.BlockSpec((1, 8, d), lambda bi, i: (bi, 0, 0)),
                _resident((1, d)), _resident(wo.shape), layer_w(w1), layer_w(w2)]
    args = [x, a, mod, g2, wo, w1, w2]
    if oscale is not None:
        in_specs.append(_resident((1, d)))
        args.append(oscale)
    return pl.pallas_call(
        functools.partial(_post_kernel, oscale is not None),
        out_shape=jax.ShapeDtypeStruct(x.shape, F32),
        grid=(b, t // tm), in_specs=in_specs, out_specs=tile(),
        compiler_params=_cparams(2), name="outproj_mlp",
    )(*args)


def _glu_kernel(x_ref, mod_ref, g_ref, w_ref, o_ref):
    mod = mod_ref[0]
    h = _norm_mod(x_ref[0], g_ref[...], mod[0:1], mod[1:2]).astype(BF)
    u = jnp.dot(h, w_ref[...], preferred_element_type=F32)
    o_ref[0] = u[:, :D_MODEL] * jax.nn.sigmoid(u[:, D_MODEL:])


def _glu_call(x, mod, g, w):
    b, t, d = x.shape
    tm = min(t, 512)
    tile = lambda: pl.BlockSpec((1, tm, d), lambda bi, i: (bi, i, 0))
    return pl.pallas_call(
        _glu_kernel, out_shape=jax.ShapeDtypeStruct(x.shape, F32), grid=(b, t // tm),
        in_specs=[tile(), pl.BlockSpec((1, 8, d), lambda bi, i: (bi, 0, 0)), _resident((1, d)), _resident(w.shape)],
        out_specs=tile(), compiler_params=_cparams(2), name="conv_pw1_glu",
    )(x, mod, g, w)


def _halo_specs(t, tm, d):
    r = tm // HALO
    last = t // HALO - 1
    return [pl.BlockSpec((1, tm, d), lambda bi, i: (bi, i, 0)),
            pl.BlockSpec((1, HALO, d), lambda bi, i: (bi, jnp.maximum(i * r - 1, 0), 0)),
            pl.BlockSpec((1, HALO, d), lambda bi, i: (bi, jnp.minimum((i + 1) * r, last), 0))]


def _dwconv_kernel(tm, u_ref, up_ref, un_ref, w_ref, b_ref, lg_ref, lb_ref, o_ref, win_ref, y_ref):
    i = pl.program_id(1)
    n = pl.num_programs(1)
    win_ref[0:HALO] = jnp.where(i > 0, up_ref[0], 0.0)
    win_ref[HALO:HALO + tm] = u_ref[0]
    win_ref[HALO + tm:] = jnp.where(i < n - 1, un_ref[0], 0.0)
    rc, cc = 128, 128
    span = rc + 2 * HALO
    for c0 in range(0, D_MODEL, cc):
        for r0 in range(0, tm, rc):
            x = win_ref[r0:r0 + span, c0:c0 + cc]
            acc = jnp.broadcast_to(b_ref[:, c0:c0 + cc], (rc, cc))
            for sub in range(8):
                xs = x if sub == 0 else pltpu.roll(x, span - sub, axis=0)
                for off in range(sub, 2 * HALO, 8):
                    j = off - (HALO - CONV_PAD)
                    if 0 <= j < CONV_WIDTH:
                        acc = acc + xs[off - sub:off - sub + rc] * w_ref[pl.ds(j, 1), c0:c0 + cc]
            y_ref[r0:r0 + rc, c0:c0 + cc] = acc
    y = y_ref[...]
    mu = jnp.mean(y, axis=-1, keepdims=True)
    yc = y - mu
    var = jnp.mean(yc * yc, axis=-1, keepdims=True)
    z = yc * lax.rsqrt(var + NORM_EPS) * lg_ref[...] + lb_ref[...]
    o_ref[0] = (z * jax.nn.sigmoid(z)).astype(BF)


def _dwconv_call(u, w_dw, b_dw, ln_g, ln_b):
    b, t, d = u.shape
    tm = 128
    return pl.pallas_call(
        functools.partial(_dwconv_kernel, tm),
        out_shape=jax.ShapeDtypeStruct(u.shape, BF), grid=(b, t // tm),
        in_specs=_halo_specs(t, tm, d) + [_resident(w_dw.shape), _resident((1, d)), _resident((1, d)), _resident((1, d))],
        out_specs=pl.BlockSpec((1, tm, d), lambda bi, i: (bi, i, 0)),
        scratch_shapes=[pltpu.VMEM((tm + 2 * HALO, d), F32), pltpu.VMEM((tm, d), F32)],
        compiler_params=_cparams(2), name="dwconv_ln_silu",
    )(u, u, u, w_dw, b_dw, ln_g, ln_b)


def _pool_kernel(tm, t_total, x_ref, xp_ref, xn_ref, mod_ref, g_ref, o_ref, win_ref):
    i = pl.program_id(1)
    n = pl.num_programs(1)
    mod = mod_ref[0]
    nm = lambda v: _norm_mod(v, g_ref[...], mod[0:1], mod[1:2])
    win_ref[0:HALO] = jnp.where(i > 0, nm(xp_ref[0]), 0.0)
    win_ref[HALO:HALO + tm] = nm(x_ref[0])
    win_ref[HALO + tm:] = jnp.where(i < n - 1, nm(xn_ref[0]), 0.0)
    pos = i * tm + lax.broadcasted_iota(jnp.int32, (tm, 1), 0)
    for gi, w in enumerate(POOL_WINDOWS):
        c0 = gi * POOL_GROUP
        half = w // 2
        acc = win_ref[pl.ds(HALO - half, tm), c0:c0 + POOL_GROUP]
        for o in range(-half + 1, w - half):
            acc = acc + win_ref[pl.ds(HALO + o, tm), c0:c0 + POOL_GROUP]
        lo = jnp.maximum(pos - half, 0)
        hi = jnp.minimum(pos + (w - half), t_total)
        cnt = (hi - lo).astype(F32)
        o_ref[0, :, c0:c0 + POOL_GROUP] = (acc / cnt - win_ref[HALO:HALO + tm, c0:c0 + POOL_GROUP]).astype(BF)


def _pool_call(x, mod, g):
    b, t, d = x.shape
    tm = 256
    return pl.pallas_call(
        functools.partial(_pool_kernel, tm, t),
        out_shape=jax.ShapeDtypeStruct(x.shape, BF), grid=(b, t // tm),
        in_specs=_halo_specs(t, tm, d) + [pl.BlockSpec((1, 8, d), lambda bi, i: (bi, 0, 0)), _resident((1, d))],
        out_specs=pl.BlockSpec((1, tm, d), lambda bi, i: (bi, i, 0)),
        scratch_shapes=[pltpu.VMEM((tm + 2 * HALO, d), F32)],
        compiler_params=_cparams(2), name="pool_windows",
    )(x, x, x, mod, g)


def _gqa_proj_kernel(rope, x_ref, mod_ref, g_ref, wt_ref, qg_ref, kg_ref, *rest):
    if rope:
        cos_ref, sin_ref, q_ref, k_ref, v_ref, y_ref = rest
    else:
        q_ref, k_ref, v_ref, y_ref = rest
    mod = mod_ref[0]
    h = _norm_mod(x_ref[0], g_ref[...], mod[0:1], mod[1:2]).astype(BF)
    for r0 in range(0, y_ref.shape[0], PROJ_ROW_CHUNK):
        y_ref[r0:r0 + PROJ_ROW_CHUNK] = lax.dot_general(wt_ref[r0:r0 + PROJ_ROW_CHUNK], h, (((1,), (1,)), ((), ())),
                                                        preferred_element_type=F32)
    hd = GQA_HEAD_DIM
    nq = GQA_HEADS * hd
    nk = GQA_KV_HEADS * hd

    def head(row0, gain_ref):
        z = _rms_rows(y_ref[row0:row0 + hd], gain_ref[...])
        if rope:
            z = _rope_rows(z, cos_ref[...], sin_ref[...], hd // 4)
        return z

    for hh in range(GQA_HEADS):
        q_ref[0, hh] = (head(hh * hd, qg_ref) * (hd ** -0.5 * LOG2E)).astype(BF)
    for hh in range(GQA_KV_HEADS):
        k_ref[0, hh] = head(nq + hh * hd, kg_ref).T.astype(BF)
        v_ref[0, hh, 0, 0:V_DIM] = y_ref[nq + nk + hh * hd:nq + nk + (hh + 1) * hd].astype(BF)
        v_ref[0, hh, 0, V_DIM:V_ROWS] = jnp.ones((V_ROWS - V_DIM, v_ref.shape[-1]), BF)


def _gqa_proj_call(x, mod, g, wt, qg, kg, tables):
    b, t, d = x.shape
    tm = min(t, 512)
    nt = t // tm
    hd = GQA_HEAD_DIM
    rope = tables is not None
    in_specs = [pl.BlockSpec((1, tm, d), lambda bi, i: (bi, i, 0)), pl.BlockSpec((1, 8, d), lambda bi, i: (bi, 0, 0)),
                _resident((1, d)), _resident(wt.shape), _resident((hd, 1)), _resident((hd, 1))]
    args = [x, mod, g, wt, qg, kg]
    if rope:
        in_specs += [pl.BlockSpec((hd, tm), lambda bi, i: (0, i))] * 2
        args += list(tables)
    return pl.pallas_call(
        functools.partial(_gqa_proj_kernel, rope),
        out_shape=(jax.ShapeDtypeStruct((b, GQA_HEADS, hd, t), BF),
                   jax.ShapeDtypeStruct((b, GQA_KV_HEADS, t, hd), BF),
                   jax.ShapeDtypeStruct((b, GQA_KV_HEADS, nt, V_ROWS, tm), BF)),
        grid=(b, nt), in_specs=in_specs,
        out_specs=(pl.BlockSpec((1, GQA_HEADS, hd, tm), lambda bi, i: (bi, 0, 0, i)),
                   pl.BlockSpec((1, GQA_KV_HEADS, tm, hd), lambda bi, i: (bi, 0, i, 0)),
                   pl.BlockSpec((1, GQA_KV_HEADS, 1, V_ROWS, tm), lambda bi, i: (bi, 0, i, 0, 0))),
        scratch_shapes=[pltpu.VMEM((wt.shape[0], tm), F32)],
        compiler_params=_cparams(2), name="gqa_qkv_proj",
    )(*args)


def _mla_proj_kernel(rope, need_q, x_ref, mod_ref, g_ref, wdq_ref, qln_ref, wuq_ref, wdkv_ref, kvln_ref, wukv_ref,
                     qg_ref, kg_ref, *rest):
    rest = list(rest)
    if rope:
        cos_ref, sin_ref = rest[:2]
        rest = rest[2:]
    if need_q:
        q_ref, k_ref, v_ref, yq_ref, ykv_ref = rest
    else:
        k_ref, v_ref, ykv_ref = rest
    mod = mod_ref[0]
    nt_dims = (((1,), (1,)), ((), ()))
    h = _norm_mod(x_ref[0], g_ref[...], mod[0:1], mod[1:2]).astype(BF)

    def tail_rope(z):
        if not rope:
            return z
        return jnp.concatenate([z[:MLA_NOPE], _rope_rows(z[MLA_NOPE:], cos_ref[...], sin_ref[...], MLA_ROPE // 4)], axis=0)

    if need_q:
        cq = lax.dot_general(wdq_ref[...], h, nt_dims, preferred_element_type=F32)
        cq = _rms_rows(cq, qln_ref[...]).astype(BF)
        for r0 in range(0, yq_ref.shape[0], PROJ_ROW_CHUNK):
            yq_ref[r0:r0 + PROJ_ROW_CHUNK] = jnp.dot(wuq_ref[r0:r0 + PROJ_ROW_CHUNK], cq, preferred_element_type=F32)
        for hh in range(MLA_HEADS):
            z = tail_rope(_rms_rows(yq_ref[hh * MLA_QK:(hh + 1) * MLA_QK], qg_ref[...]))
            q_ref[0, hh] = (z * (MLA_QK ** -0.5 * LOG2E)).astype(BF)
    dkv = lax.dot_general(wdkv_ref[...], h, nt_dims, preferred_element_type=F32)
    ckv = _rms_rows(dkv[:MLA_KV_LORA], kvln_ref[...]).astype(BF)
    k_rope = dkv[MLA_KV_LORA:]
    for r0 in range(0, ykv_ref.shape[0], PROJ_ROW_CHUNK):
        ykv_ref[r0:r0 + PROJ_ROW_CHUNK] = jnp.dot(wukv_ref[r0:r0 + PROJ_ROW_CHUNK], ckv, preferred_element_type=F32)
    per = MLA_NOPE + MLA_V
    for hh in range(MLA_HEADS):
        kf = jnp.concatenate([ykv_ref[hh * per:hh * per + MLA_NOPE], k_rope], axis=0)
        k_ref[0, hh] = tail_rope(_rms_rows(kf, kg_ref[...])).T.astype(BF)
        v_ref[0, hh, 0, 0:V_DIM] = ykv_ref[hh * per + MLA_NOPE:(hh + 1) * per].astype(BF)
        v_ref[0, hh, 0, V_DIM:V_ROWS] = jnp.ones((V_ROWS - V_DIM, v_ref.shape[-1]), BF)


def _mla_proj_call(x, mod, g, wdq_t, qln, wuq_t, wdkv_t, kvln, wukv_t, qg, kg, tables, need_q):
    b, t, d = x.shape
    tm = min(t, 512)
    nt = t // tm
    rope = tables is not None
    in_specs = [pl.BlockSpec((1, tm, d), lambda bi, i: (bi, i, 0)), pl.BlockSpec((1, 8, d), lambda bi, i: (bi, 0, 0)),
                _resident((1, d)), _resident(wdq_t.shape), _resident(qln.shape), _resident(wuq_t.shape),
                _resident(wdkv_t.shape), _resident(kvln.shape), _resident(wukv_t.shape),
                _resident(qg.shape), _resident(kg.shape)]
    args = [x, mod, g, wdq_t, qln, wuq_t, wdkv_t, kvln, wukv_t, qg, kg]
    if rope:
        in_specs += [pl.BlockSpec((MLA_ROPE, tm), lambda bi, i: (0, i))] * 2
        args += list(tables)
    out_shape = [jax.ShapeDtypeStruct((b, MLA_HEADS, t, MLA_QK), BF),
                 jax.ShapeDtypeStruct((b, MLA_HEADS, nt, V_ROWS, tm), BF)]
    out_specs = [pl.BlockSpec((1, MLA_HEADS, tm, MLA_QK), lambda bi, i: (bi, 0, i, 0)),
                 pl.BlockSpec((1, MLA_HEADS, 1, V_ROWS, tm), lambda bi, i: (bi, 0, i, 0, 0))]
    scratch = [pltpu.VMEM((wukv_t.shape[0], tm), F32)]
    if need_q:
        out_shape.insert(0, jax.ShapeDtypeStruct((b, MLA_HEADS, MLA_QK, t), BF))
        out_specs.insert(0, pl.BlockSpec((1, MLA_HEADS, MLA_QK, tm), lambda bi, i: (bi, 0, 0, i)))
        scratch.insert(0, pltpu.VMEM((wuq_t.shape[0], tm), F32))
    return pl.pallas_call(
        functools.partial(_mla_proj_kernel, rope, need_q),
        out_shape=tuple(out_shape), grid=(b, nt), in_specs=in_specs, out_specs=tuple(out_specs),
        scratch_shapes=scratch, compiler_params=_cparams(2), name="mla_proj",
    )(*args)


def _attn_kernel(kv_per_step, n_lat, tq, group, q_ref, kc_ref, vc_ref, *rest):
    if n_lat:
        kl_ref, vl_ref, o_ref, m_ref, acc_ref, *bufs = rest
    else:
        o_ref, m_ref, acc_ref, *bufs = rest
    kv_of = lambda hh: hh * kv_per_step // HEADS_PER_STEP
    lc = kc_ref.shape[3]
    tk = kl_ref.shape[3] if n_lat else 0
    n_groups = q_ref.shape[3] // (tq * group)
    per_group = group * HEADS_PER_STEP

    for gi in range(n_groups):
        s0, s1, p0, p1, a0, a1, x0, x1 = bufs[8 * (gi % 2):8 * (gi % 2 + 1)]
        s_buf, p_buf, a_buf, x_buf = (s0, s1), (p0, p1), (a0, a1), (x0, x1)
        streams = [(slice((gi * group + t) * tq, (gi * group + t + 1) * tq), hh, gi * per_group + t * HEADS_PER_STEP + hh)
                   for t in range(group) for hh in range(HEADS_PER_STEP)]

        def stage_s(c, n):
            rows = lc if n is None else tk
            for e, (cols, hh, _) in enumerate(streams):
                k = kc_ref[0, kv_of(hh), 0] if n is None else kl_ref[0, kv_of(hh), n]
                s = jnp.dot(k, q_ref[0, hh, :, cols], preferred_element_type=F32)
                s_buf[c][e, 0:rows] = s
                x_buf[c][e] = jnp.max(s, axis=0, keepdims=True)

        def stage_sm(c, rows):
            for e, (_, _, st) in enumerate(streams):
                m_old = m_ref[st]
                m_new = jnp.maximum(m_old, x_buf[c][e])
                m_ref[st] = m_new
                a_buf[c][e] = jnp.exp2(m_old - m_new)
                p_buf[c][e, 0:rows] = jnp.exp2(s_buf[c][e, 0:rows] - m_new).astype(BF)

        def stage_pv(c, n):
            rows = lc if n is None else tk
            for e, (_, hh, st) in enumerate(streams):
                v = vc_ref[0, kv_of(hh), 0] if n is None else vl_ref[0, kv_of(hh), n]
                pv = jnp.dot(v, p_buf[c][e, 0:rows], preferred_element_type=F32)
                acc_ref[st] = a_buf[c][e] * acc_ref[st] + pv

        for _, _, st in streams:
            m_ref[st] = jnp.full(m_ref.shape[1:], -1e30, F32)
            acc_ref[st] = jnp.zeros(acc_ref.shape[1:], F32)
        stage_s(0, None)
        if n_lat:
            stage_s(1, 0)
            stage_sm(0, lc)
            stage_s(0, 1)
            stage_sm(1, tk)
            stage_pv(0, None)
            for j in range(1, n_lat - 1, 2):
                stage_s(1, j + 1)
                stage_sm(0, tk)
                stage_pv(1, j - 1)
                stage_s(0, j + 2)
                stage_sm(1, tk)
                stage_pv(0, j)
            stage_sm(0, tk)
            stage_pv(1, n_lat - 2)
            stage_pv(0, n_lat - 1)
        else:
            stage_sm(0, lc)
            stage_pv(0, None)
        for t in range(group):
            tile = streams[t * HEADS_PER_STEP:(t + 1) * HEADS_PER_STEP]
            outs = [acc_ref[st, 0:V_DIM] / acc_ref[st, V_DIM:V_DIM + 1] for _, _, st in tile]
            o_ref[0, tile[0][0], :] = jnp.concatenate(outs, axis=0).T.astype(BF)


def _attn_call(qt, kc, vct, kl=None, vlt=None):
    b, nh, dq, t = qt.shape
    hkv = kc.shape[1]
    kvps = HEADS_PER_STEP * hkv // nh
    tq = ATTN_TQ
    tstep = min(t, ATTN_TILES_PER_STEP * tq)
    n_tiles = tstep // tq
    group = min(ATTN_TILE_GROUP, n_tiles)
    n_lat = 0 if kl is None else kl.shape[2]
    assert n_lat % 2 == 0, "the skewed key-block schedule handles latent blocks in pairs"
    rows_max = kc.shape[3] if kl is None else max(kc.shape[3], kl.shape[3])
    kvspec = lambda a: pl.BlockSpec((1, kvps) + a.shape[2:], lambda bi, j, i: (bi, j, 0, 0, 0))
    in_specs = [pl.BlockSpec((1, HEADS_PER_STEP, dq, tstep), lambda bi, j, i: (bi, j, 0, i)), kvspec(kc), kvspec(vct)]
    args = [qt, kc, vct]
    if n_lat:
        in_specs += [kvspec(kl), kvspec(vlt)]
        args += [kl, vlt]
    streams = group * HEADS_PER_STEP
    buffer_set = ([pltpu.VMEM((streams, rows_max, tq), F32)] * 2 + [pltpu.VMEM((streams, rows_max, tq), BF)] * 2
                  + [pltpu.VMEM((streams, 1, tq), F32)] * 4)
    return pl.pallas_call(
        functools.partial(_attn_kernel, kvps, n_lat, tq, group),
        out_shape=jax.ShapeDtypeStruct((b, t, nh * V_DIM), BF),
        grid=(b, nh // HEADS_PER_STEP, t // tstep), in_specs=in_specs,
        out_specs=pl.BlockSpec((1, tstep, HEADS_PER_STEP * V_DIM), lambda bi, j, i: (bi, i, j)),
        scratch_shapes=[pltpu.VMEM((n_tiles * HEADS_PER_STEP, 1, tq), F32),
                        pltpu.VMEM((n_tiles * HEADS_PER_STEP, V_ROWS, tq), F32)]
        + buffer_set * min(n_tiles // group, 2),
        compiler_params=_cparams(3), name="attention",
    )(*args)


def _rope_tables(seq, rot_dim):
    quarter = rot_dim // 4
    pos = jnp.arange(seq)
    row = (pos // GRID_W).astype(F32)
    col = (pos % GRID_W).astype(F32)
    inv = ROPE_THETA ** (-jnp.arange(quarter, dtype=F32) / quarter)
    ang_r = inv[:, None] * row[None, :]
    ang_c = inv[:, None] * col[None, :]
    cos = jnp.concatenate([jnp.cos(ang_r)] * 2 + [jnp.cos(ang_c)] * 2, axis=0)
    sin = jnp.concatenate([-jnp.sin(ang_r), jnp.sin(ang_r), -jnp.sin(ang_c), jnp.sin(ang_c)], axis=0)
    return cos, sin


def _col(v):
    return v.reshape(-1, 1)


def _row(v):
    return v.reshape(1, -1)


def kernel(x, c, ctx, c_ctx, norm1_g, norm2_g, w_mod, b_mod, w_ff1, w_ff2, conv_w_pw1, conv_w_dw, conv_b_dw, conv_ln_g, conv_ln_b, conv_w_pw2, gqa_w_qkv, gqa_q_norm, gqa_k_norm, gqa_w_o, pool_w, pool_scale, mla_w_dq, mla_q_lora_norm, mla_w_uq, mla_w_dkv, mla_kv_lora_norm, mla_w_ukv, mla_q_norm, mla_k_norm, mla_w_o):
    batch, seq, d = x.shape
    depth = w_mod.shape[0]

    cvec = jnp.concatenate([c, c_ctx[None], jnp.zeros((8 - batch - 1, d), F32)], axis=0)
    mod = _mod_call(cvec, w_mod, b_mod).reshape(depth, 8, 6, d)
    mod = jnp.pad(mod, ((0, 0), (0, 0), (0, 2), (0, 0)))
    mod_l = mod[:, :batch]
    mod_c = jnp.broadcast_to(mod[:, batch:batch + 1], (depth, batch, 8, d))

    tab_gqa = _rope_tables(seq, GQA_HEAD_DIM)
    tab_mla = _rope_tables(seq, MLA_ROPE)

    def split_kv(k, vt):
        b_, h_, l_, dq_ = k.shape
        n_ = vt.shape[2]
        return k.reshape(b_, h_, n_, l_ // n_, dq_), vt

    flat = lambda a: a.reshape(1, -1, d)
    unflat = lambda a: a.reshape(batch, -1, d)

    w1_all = w_ff1.astype(BF)
    w2_all = w_ff2.astype(BF)
    xl, xc = x, ctx
    for i in range(depth):
        m = i % N_MIXERS
        j = i // N_MIXERS
        with_ctx = i < depth - 1
        g1 = _row(norm1_g[i])
        g2 = _row(norm2_g[i])
        ml, mc = mod_l[i], mod_c[i]
        oscale = None
        if m == 0:
            wp1 = conv_w_pw1[j].astype(BF)
            wo = conv_w_pw2[j].astype(BF)
            cargs = (jnp.pad(conv_w_dw[j], ((0, 1), (0, 0))), _row(conv_b_dw[j]), _row(conv_ln_g[j]), _row(conv_ln_b[j]))
            al = _dwconv_call(_glu_call(xl, ml, g1, wp1), *cargs)
            ac = _dwconv_call(unflat(_glu_call(flat(xc), mc[:1], g1, wp1)), *cargs) if with_ctx else None
        elif m == 1:
            wt = gqa_w_qkv[j].T.astype(BF)
            wo = gqa_w_o[j].astype(BF)
            qg, kg = _col(gqa_q_norm[j]), _col(gqa_k_norm[j])
            ql, kl, vl = _gqa_proj_call(xl, ml, g1, wt, qg, kg, tab_gqa)
            qc, kc, vc = _gqa_proj_call(xc, mc, g1, wt, qg, kg, None)
            kc5, vc5 = split_kv(kc, vc)
            al = _attn_call(ql, kc5, vc5, *split_kv(kl, vl))
            ac = _attn_call(qc, kc5, vc5) if with_ctx else None
        elif m == 2:
            eye = jnp.eye(len(POOL_WINDOWS), dtype=F32)
            wo = jnp.einsum('gcd,gh->gchd', pool_w[j], eye).reshape(d, d).astype(BF)
            oscale = _row(pool_scale[j])
            al = _pool_call(xl, ml, g1)
            ac = _pool_call(xc, mc, g1) if with_ctx else None
        else:
            wo = mla_w_o[j].astype(BF)
            pargs = (mla_w_dq[j].T.astype(BF), _col(mla_q_lora_norm[j]), mla_w_uq[j].T.astype(BF),
                     mla_w_dkv[j].T.astype(BF), _col(mla_kv_lora_norm[j]), mla_w_ukv[j].T.astype(BF),
                     _col(mla_q_norm[j]), _col(mla_k_norm[j]))
            ql, kl, vl = _mla_proj_call(xl, ml, g1, *pargs, tab_mla, True)
            pc = _mla_proj_call(xc, mc, g1, *pargs, None, with_ctx)
            kc5, vc5 = split_kv(*pc[-2:])
            al = _attn_call(ql, kc5, vc5, *split_kv(kl, vl))
            ac = _attn_call(pc[0], kc5, vc5) if with_ctx else None
        xl = _post_call(xl, al, ml, g2, wo, w1_all, w2_all, i, oscale)
        if with_ctx:
            xc = unflat(_post_call(flat(xc), flat(ac), mc[:1], g2, wo, w1_all, w2_all, i, oscale))
    return xl
```

```python
import functools

import jax
import jax.numpy as jnp
from jax import lax
from jax.experimental import pallas as pl
from jax.experimental.pallas import tpu as pltpu

D_MODEL = 1024
GRID_W = 64
ROPE_THETA = 10000.0
NORM_EPS = 1e-6
CONV_WIDTH = 31
CONV_PAD = CONV_WIDTH // 2
GQA_HEADS = 16
GQA_KV_HEADS = 4
GQA_HEAD_DIM = 64
POOL_WINDOWS = (2, 4, 8, 16)
POOL_GROUP = 256
MLA_HEADS = 16
MLA_NOPE = 64
MLA_ROPE = 32
MLA_V = 64
MLA_QK = MLA_NOPE + MLA_ROPE
MLA_Q_LORA = 768
MLA_KV_LORA = 256
D_FF = 4 * D_MODEL
N_MIXERS = 4

HALO = 16
HEADS_PER_STEP = 4
V_DIM = 64
PROJ_ROW_CHUNK = 512
ATTN_TQ = 256
ATTN_TILES_PER_STEP = 4
ATTN_TILE_GROUP = 2
V_ROWS = V_DIM + 16
VMEM_LIMIT = 52 * 1024 * 1024
LOG2E = 1.4426950408889634
BF = jnp.bfloat16
F32 = jnp.float32


def _cparams(n_axes):
    return pltpu.CompilerParams(dimension_semantics=("arbitrary",) * n_axes, vmem_limit_bytes=VMEM_LIMIT)


def _resident(shape):
    nd = len(shape)
    return pl.BlockSpec(shape, lambda *_: (0,) * nd, pipeline_mode=pl.Buffered(1))


def _norm_mod(x, g, shift, scale):
    ms = jnp.mean(x * x, axis=-1, keepdims=True)
    y = x * lax.rsqrt(ms + NORM_EPS) * g
    return y * (1.0 + scale) + shift


def _rms_rows(xt, gain):
    ms = jnp.mean(xt * xt, axis=0, keepdims=True)
    return xt * lax.rsqrt(ms + NORM_EPS) * gain


def _rope_rows(xt, cos, sin, quarter):
    q = quarter
    rot = jnp.concatenate([xt[q:2 * q], xt[0:q], xt[3 * q:4 * q], xt[2 * q:3 * q]], axis=0)
    return xt * cos + rot * sin


def _mod_kernel(c_ref, w_ref, b_ref, o_ref):
    c = c_ref[...]
    s = (c * jax.nn.sigmoid(c)).astype(BF)
    o_ref[0] = jnp.dot(s, w_ref[0].astype(BF), preferred_element_type=F32) + b_ref[0]


def _mod_call(cvec, w_mod, b_mod):
    depth, d, n = w_mod.shape
    rows = cvec.shape[0]
    tn = 1536
    return pl.pallas_call(
        _mod_kernel,
        out_shape=jax.ShapeDtypeStruct((depth, rows, n), F32),
        grid=(depth, n // tn),
        in_specs=[pl.BlockSpec((rows, d), lambda l, j: (0, 0)),
                  pl.BlockSpec((1, d, tn), lambda l, j: (l, 0, j)),
                  pl.BlockSpec((1, 1, tn), lambda l, j: (l, 0, j))],
        out_specs=pl.BlockSpec((1, rows, tn), lambda l, j: (l, 0, j)),
        compiler_params=_cparams(2), name="adaln_mod",
    )(cvec, w_mod, b_mod.reshape(depth, 1, n))


def _post_kernel(has_oscale, x_ref, a_ref, mod_ref, g2_ref, wo_ref, w1_ref, w2_ref, *rest):
    if has_oscale:
        os_ref, o_ref = rest
    else:
        (o_ref,) = rest
    mod = mod_ref[0]
    y = jnp.dot(a_ref[0], wo_ref[...], preferred_element_type=F32)
    if has_oscale:
        y = y * os_ref[...]
    x1 = x_ref[0] + mod[2:3] * y
    h = _norm_mod(x1, g2_ref[...], mod[3:4], mod[4:5]).astype(BF)
    ffc = D_MODEL
    acc = jnp.zeros_like(x1)
    for c in range(D_FF // ffc):
        t = jnp.dot(h, w1_ref[0, :, c * ffc:(c + 1) * ffc], preferred_element_type=F32)
        t = jnp.maximum(t, 0.0)
        acc = acc + jnp.dot((t * t).astype(BF), w2_ref[0, c * ffc:(c + 1) * ffc, :], preferred_element_type=F32)
    o_ref[0] = x1 + mod[5:6] * acc


def _post_call(x, a, mod, g2, wo, w1, w2, layer, oscale=None):
    b, t, d = x.shape
    tm = min(t, 512)
    tile = lambda: pl.BlockSpec((1, tm, d), lambda bi, i: (bi, i, 0))
    layer_w = lambda w: pl.BlockSpec((1,) + w.shape[1:], lambda bi, i: (layer, 0, 0), pipeline_mode=pl.Buffered(1))
    in_specs = [tile(), tile(), pl.BlockSpec((1, 8, d), lambda bi, i: (bi, 0, 0)),
                _resident((1, d)), _resident(wo.shape), layer_w(w1), layer_w(w2)]
    args = [x, a, mod, g2, wo, w1, w2]
    if oscale is not None:
        in_specs.append(_resident((1, d)))
        args.append(oscale)
    return pl.pallas_call(
        functools.partial(_post_kernel, oscale is not None),
        out_shape=jax.ShapeDtypeStruct(x.shape, F32),
        grid=(b, t // tm), in_specs=in_specs, out_specs=tile(),
        compiler_params=_cparams(2), name="outproj_mlp",
    )(*args)


def _glu_kernel(x_ref, mod_ref, g_ref, w_ref, o_ref):
    mod = mod_ref[0]
    h = _norm_mod(x_ref[0], g_ref[...], mod[0:1], mod[1:2]).astype(BF)
    u = jnp.dot(h, w_ref[...], preferred_element_type=F32)
    o_ref[0] = u[:, :D_MODEL] * jax.nn.sigmoid(u[:, D_MODEL:])


def _glu_call(x, mod, g, w):
    b, t, d = x.shape
    tm = min(t, 512)
    tile = lambda: pl.BlockSpec((1, tm, d), lambda bi, i: (bi, i, 0))
    return pl.pallas_call(
        _glu_kernel, out_shape=jax.ShapeDtypeStruct(x.shape, F32), grid=(b, t // tm),
        in_specs=[tile(), pl.BlockSpec((1, 8, d), lambda bi, i: (bi, 0, 0)), _resident((1, d)), _resident(w.shape)],
        out_specs=tile(), compiler_params=_cparams(2), name="conv_pw1_glu",
    )(x, mod, g, w)


def _halo_specs(t, tm, d):
    r = tm // HALO
    last = t // HALO - 1
    return [pl.BlockSpec((1, tm, d), lambda bi, i: (bi, i, 0)),
            pl.BlockSpec((1, HALO, d), lambda bi, i: (bi, jnp.maximum(i * r - 1, 0), 0)),
            pl.BlockSpec((1, HALO, d), lambda bi, i: (bi, jnp.minimum((i + 1) * r, last), 0))]


def _dwconv_kernel(tm, u_ref, up_ref, un_ref, w_ref, b_ref, lg_ref, lb_ref, o_ref, win_ref, y_ref):
    i = pl.program_id(1)
    n = pl.num_programs(1)
    win_ref[0:HALO] = jnp.where(i > 0, up_ref[0], 0.0)
    win_ref[HALO:HALO + tm] = u_ref[0]
    win_ref[HALO + tm:] = jnp.where(i < n - 1, un_ref[0], 0.0)
    rc, cc = 128, 128
    span = rc + 2 * HALO
    for c0 in range(0, D_MODEL, cc):
        for r0 in range(0, tm, rc):
            x = win_ref[r0:r0 + span, c0:c0 + cc]
            acc = jnp.broadcast_to(b_ref[:, c0:c0 + cc], (rc, cc))
            for sub in range(8):
                xs = x if sub == 0 else pltpu.roll(x, span - sub, axis=0)
                for off in range(sub, 2 * HALO, 8):
                    j = off - (HALO - CONV_PAD)
                    if 0 <= j < CONV_WIDTH:
                        acc = acc + xs[off - sub:off - sub + rc] * w_ref[pl.ds(j, 1), c0:c0 + cc]
            y_ref[r0:r0 + rc, c0:c0 + cc] = acc
    y = y_ref[...]
    mu = jnp.mean(y, axis=-1, keepdims=True)
    yc = y - mu
    var = jnp.mean(yc * yc, axis=-1, keepdims=True)
    z = yc * lax.rsqrt(var + NORM_EPS) * lg_ref[...] + lb_ref[...]
    o_ref[0] = (z * jax.nn.sigmoid(z)).astype(BF)


def _dwconv_call(u, w_dw, b_dw, ln_g, ln_b):
    b, t, d = u.shape
    tm = 256
    return pl.pallas_call(
        functools.partial(_dwconv_kernel, tm),
        out_shape=jax.ShapeDtypeStruct(u.shape, BF), grid=(b, t // tm),
        in_specs=_halo_specs(t, tm, d) + [_resident(w_dw.shape), _resident((1, d)), _resident((1, d)), _resident((1, d))],
        out_specs=pl.BlockSpec((1, tm, d), lambda bi, i: (bi, i, 0)),
        scratch_shapes=[pltpu.VMEM((tm + 2 * HALO, d), F32), pltpu.VMEM((tm, d), F32)],
        compiler_params=_cparams(2), name="dwconv_ln_silu",
    )(u, u, u, w_dw, b_dw, ln_g, ln_b)


def _pool_kernel(tm, t_total, x_ref, xp_ref, xn_ref, mod_ref, g_ref, o_ref, win_ref):
    i = pl.program_id(1)
    n = pl.num_programs(1)
    mod = mod_ref[0]
    nm = lambda v: _norm_mod(v, g_ref[...], mod[0:1], mod[1:2])
    win_ref[0:HALO] = jnp.where(i > 0, nm(xp_ref[0]), 0.0)
    win_ref[HALO:HALO + tm] = nm(x_ref[0])
    win_ref[HALO + tm:] = jnp.where(i < n - 1, nm(xn_ref[0]), 0.0)
    pos = i * tm + lax.broadcasted_iota(jnp.int32, (tm, 1), 0)
    for gi, w in enumerate(POOL_WINDOWS):
        c0 = gi * POOL_GROUP
        half = w // 2
        acc = win_ref[pl.ds(HALO - half, tm), c0:c0 + POOL_GROUP]
        for o in range(-half + 1, w - half):
            acc = acc + win_ref[pl.ds(HALO + o, tm), c0:c0 + POOL_GROUP]
        lo = jnp.maximum(pos - half, 0)
        hi = jnp.minimum(pos + (w - half), t_total)
        cnt = (hi - lo).astype(F32)
        o_ref[0, :, c0:c0 + POOL_GROUP] = (acc / cnt - win_ref[HALO:HALO + tm, c0:c0 + POOL_GROUP]).astype(BF)


def _pool_call(x, mod, g):
    b, t, d = x.shape
    tm = 256
    return pl.pallas_call(
        functools.partial(_pool_kernel, tm, t),
        out_shape=jax.ShapeDtypeStruct(x.shape, BF), grid=(b, t // tm),
        in_specs=_halo_specs(t, tm, d) + [pl.BlockSpec((1, 8, d), lambda bi, i: (bi, 0, 0)), _resident((1, d))],
        out_specs=pl.BlockSpec((1, tm, d), lambda bi, i: (bi, i, 0)),
        scratch_shapes=[pltpu.VMEM((tm + 2 * HALO, d), F32)],
        compiler_params=_cparams(2), name="pool_windows",
    )(x, x, x, mod, g)


def _gqa_proj_kernel(rope, x_ref, mod_ref, g_ref, wt_ref, qg_ref, kg_ref, *rest):
    if rope:
        cos_ref, sin_ref, q_ref, k_ref, v_ref, y_ref = rest
    else:
        q_ref, k_ref, v_ref, y_ref = rest
    mod = mod_ref[0]
    h = _norm_mod(x_ref[0], g_ref[...], mod[0:1], mod[1:2]).astype(BF)
    for r0 in range(0, y_ref.shape[0], PROJ_ROW_CHUNK):
        y_ref[r0:r0 + PROJ_ROW_CHUNK] = lax.dot_general(wt_ref[r0:r0 + PROJ_ROW_CHUNK], h, (((1,), (1,)), ((), ())),
                                                        preferred_element_type=F32)
    hd = GQA_HEAD_DIM
    nq = GQA_HEADS * hd
    nk = GQA_KV_HEADS * hd

    def head(row0, gain_ref):
        z = _rms_rows(y_ref[row0:row0 + hd], gain_ref[...])
        if rope:
            z = _rope_rows(z, cos_ref[...], sin_ref[...], hd // 4)
        return z

    for hh in range(GQA_HEADS):
        q_ref[0, hh] = (head(hh * hd, qg_ref) * (hd ** -0.5 * LOG2E)).astype(BF)
    for hh in range(GQA_KV_HEADS):
        k_ref[0, hh] = head(nq + hh * hd, kg_ref).T.astype(BF)
        v_ref[0, hh, 0, 0:V_DIM] = y_ref[nq + nk + hh * hd:nq + nk + (hh + 1) * hd].astype(BF)
        v_ref[0, hh, 0, V_DIM:V_ROWS] = jnp.ones((V_ROWS - V_DIM, v_ref.shape[-1]), BF)


def _gqa_proj_call(x, mod, g, wt, qg, kg, tables):
    b, t, d = x.shape
    tm = min(t, 512)
    nt = t // tm
    hd = GQA_HEAD_DIM
    rope = tables is not None
    in_specs = [pl.BlockSpec((1, tm, d), lambda bi, i: (bi, i, 0)), pl.BlockSpec((1, 8, d), lambda bi, i: (bi, 0, 0)),
                _resident((1, d)), _resident(wt.shape), _resident((hd, 1)), _resident((hd, 1))]
    args = [x, mod, g, wt, qg, kg]
    if rope:
        in_specs += [pl.BlockSpec((hd, tm), lambda bi, i: (0, i))] * 2
        args += list(tables)
    return pl.pallas_call(
        functools.partial(_gqa_proj_kernel, rope),
        out_shape=(jax.ShapeDtypeStruct((b, GQA_HEADS, hd, t), BF),
                   jax.ShapeDtypeStruct((b, GQA_KV_HEADS, t, hd), BF),
                   jax.ShapeDtypeStruct((b, GQA_KV_HEADS, nt, V_ROWS, tm), BF)),
        grid=(b, nt), in_specs=in_specs,
        out_specs=(pl.BlockSpec((1, GQA_HEADS, hd, tm), lambda bi, i: (bi, 0, 0, i)),
                   pl.BlockSpec((1, GQA_KV_HEADS, tm, hd), lambda bi, i: (bi, 0, i, 0)),
                   pl.BlockSpec((1, GQA_KV_HEADS, 1, V_ROWS, tm), lambda bi, i: (bi, 0, i, 0, 0))),
        scratch_shapes=[pltpu.VMEM((wt.shape[0], tm), F32)],
        compiler_params=_cparams(2), name="gqa_qkv_proj",
    )(*args)


def _mla_proj_kernel(rope, need_q, x_ref, mod_ref, g_ref, wdq_ref, qln_ref, wuq_ref, wdkv_ref, kvln_ref, wukv_ref,
                     qg_ref, kg_ref, *rest):
    rest = list(rest)
    if rope:
        cos_ref, sin_ref = rest[:2]
        rest = rest[2:]
    if need_q:
        q_ref, k_ref, v_ref, yq_ref, ykv_ref = rest
    else:
        k_ref, v_ref, ykv_ref = rest
    mod = mod_ref[0]
    nt_dims = (((1,), (1,)), ((), ()))
    h = _norm_mod(x_ref[0], g_ref[...], mod[0:1], mod[1:2]).astype(BF)

    def tail_rope(z):
        if not rope:
            return z
        return jnp.concatenate([z[:MLA_NOPE], _rope_rows(z[MLA_NOPE:], cos_ref[...], sin_ref[...], MLA_ROPE // 4)], axis=0)

    if need_q:
        cq = lax.dot_general(wdq_ref[...], h, nt_dims, preferred_element_type=F32)
        cq = _rms_rows(cq, qln_ref[...]).astype(BF)
        for r0 in range(0, yq_ref.shape[0], PROJ_ROW_CHUNK):
            yq_ref[r0:r0 + PROJ_ROW_CHUNK] = jnp.dot(wuq_ref[r0:r0 + PROJ_ROW_CHUNK], cq, preferred_element_type=F32)
        for hh in range(MLA_HEADS):
            z = tail_rope(_rms_rows(yq_ref[hh * MLA_QK:(hh + 1) * MLA_QK], qg_ref[...]))
            q_ref[0, hh] = (z * (MLA_QK ** -0.5 * LOG2E)).astype(BF)
    dkv = lax.dot_general(wdkv_ref[...], h, nt_dims, preferred_element_type=F32)
    ckv = _rms_rows(dkv[:MLA_KV_LORA], kvln_ref[...]).astype(BF)
    k_rope = dkv[MLA_KV_LORA:]
    for r0 in range(0, ykv_ref.shape[0], PROJ_ROW_CHUNK):
        ykv_ref[r0:r0 + PROJ_ROW_CHUNK] = jnp.dot(wukv_ref[r0:r0 + PROJ_ROW_CHUNK], ckv, preferred_element_type=F32)
    per = MLA_NOPE + MLA_V
    for hh in range(MLA_HEADS):
        kf = jnp.concatenate([ykv_ref[hh * per:hh * per + MLA_NOPE], k_rope], axis=0)
        k_ref[0, hh] = tail_rope(_rms_rows(kf, kg_ref[...])).T.astype(BF)
        v_ref[0, hh, 0, 0:V_DIM] = ykv_ref[hh * per + MLA_NOPE:(hh + 1) * per].astype(BF)
        v_ref[0, hh, 0, V_DIM:V_ROWS] = jnp.ones((V_ROWS - V_DIM, v_ref.shape[-1]), BF)


def _mla_proj_call(x, mod, g, wdq_t, qln, wuq_t, wdkv_t, kvln, wukv_t, qg, kg, tables, need_q):
    b, t, d = x.shape
    tm = min(t, 512)
    nt = t // tm
    rope = tables is not None
    in_specs = [pl.BlockSpec((1, tm, d), lambda bi, i: (bi, i, 0)), pl.BlockSpec((1, 8, d), lambda bi, i: (bi, 0, 0)),
                _resident((1, d)), _resident(wdq_t.shape), _resident(qln.shape), _resident(wuq_t.shape),
                _resident(wdkv_t.shape), _resident(kvln.shape), _resident(wukv_t.shape),
                _resident(qg.shape), _resident(kg.shape)]
    args = [x, mod, g, wdq_t, qln, wuq_t, wdkv_t, kvln, wukv_t, qg, kg]
    if rope:
        in_specs += [pl.BlockSpec((MLA_ROPE, tm), lambda bi, i: (0, i))] * 2
        args += list(tables)
    out_shape = [jax.ShapeDtypeStruct((b, MLA_HEADS, t, MLA_QK), BF),
                 jax.ShapeDtypeStruct((b, MLA_HEADS, nt, V_ROWS, tm), BF)]
    out_specs = [pl.BlockSpec((1, MLA_HEADS, tm, MLA_QK), lambda bi, i: (bi, 0, i, 0)),
                 pl.BlockSpec((1, MLA_HEADS, 1, V_ROWS, tm), lambda bi, i: (bi, 0, i, 0, 0))]
    scratch = [pltpu.VMEM((wukv_t.shape[0], tm), F32)]
    if need_q:
        out_shape.insert(0, jax.ShapeDtypeStruct((b, MLA_HEADS, MLA_QK, t), BF))
        out_specs.insert(0, pl.BlockSpec((1, MLA_HEADS, MLA_QK, tm), lambda bi, i: (bi, 0, 0, i)))
        scratch.insert(0, pltpu.VMEM((wuq_t.shape[0], tm), F32))
    return pl.pallas_call(
        functools.partial(_mla_proj_kernel, rope, need_q),
        out_shape=tuple(out_shape), grid=(b, nt), in_specs=in_specs, out_specs=tuple(out_specs),
        scratch_shapes=scratch, compiler_params=_cparams(2), name="mla_proj",
    )(*args)


def _attn_kernel(kv_per_step, n_lat, tq, group, q_ref, kc_ref, vc_ref, *rest):
    if n_lat:
        kl_ref, vl_ref, o_ref, m_ref, acc_ref, *bufs = rest
    else:
        o_ref, m_ref, acc_ref, *bufs = rest
    kv_of = lambda hh: hh * kv_per_step // HEADS_PER_STEP
    lc = kc_ref.shape[3]
    tk = kl_ref.shape[3] if n_lat else 0
    n_groups = q_ref.shape[3] // (tq * group)
    per_group = group * HEADS_PER_STEP

    for gi in range(n_groups):
        s0, s1, p0, p1, a0, a1, x0, x1 = bufs[8 * (gi % 2):8 * (gi % 2 + 1)]
        s_buf, p_buf, a_buf, x_buf = (s0, s1), (p0, p1), (a0, a1), (x0, x1)
        streams = [(slice((gi * group + t) * tq, (gi * group + t + 1) * tq), hh, gi * per_group + t * HEADS_PER_STEP + hh)
                   for t in range(group) for hh in range(HEADS_PER_STEP)]

        def stage_s(c, n):
            rows = lc if n is None else tk
            for e, (cols, hh, _) in enumerate(streams):
                k = kc_ref[0, kv_of(hh), 0] if n is None else kl_ref[0, kv_of(hh), n]
                s = jnp.dot(k, q_ref[0, hh, :, cols], preferred_element_type=F32)
                s_buf[c][e, 0:rows] = s
                x_buf[c][e] = jnp.max(s, axis=0, keepdims=True)

        def stage_sm(c, rows):
            for e, (_, _, st) in enumerate(streams):
                m_old = m_ref[st]
                m_new = jnp.maximum(m_old, x_buf[c][e])
                m_ref[st] = m_new
                a_buf[c][e] = jnp.exp2(m_old - m_new)
                p_buf[c][e, 0:rows] = jnp.exp2(s_buf[c][e, 0:rows] - m_new).astype(BF)

        def stage_pv(c, n):
            rows = lc if n is None else tk
            for e, (_, hh, st) in enumerate(streams):
                v = vc_ref[0, kv_of(hh), 0] if n is None else vl_ref[0, kv_of(hh), n]
                pv = jnp.dot(v, p_buf[c][e, 0:rows], preferred_element_type=F32)
                acc_ref[st] = a_buf[c][e] * acc_ref[st] + pv

        for _, _, st in streams:
            m_ref[st] = jnp.full(m_ref.shape[1:], -1e30, F32)
            acc_ref[st] = jnp.zeros(acc_ref.shape[1:], F32)
        stage_s(0, None)
        if n_lat:
            stage_s(1, 0)
            stage_sm(0, lc)
            stage_s(0, 1)
            stage_sm(1, tk)
            stage_pv(0, None)
            for j in range(1, n_lat - 1, 2):
                stage_s(1, j + 1)
                stage_sm(0, tk)
                stage_pv(1, j - 1)
                stage_s(0, j + 2)
                stage_sm(1, tk)
                stage_pv(0, j)
            stage_sm(0, tk)
            stage_pv(1, n_lat - 2)
            stage_pv(0, n_lat - 1)
        else:
            stage_sm(0, lc)
            stage_pv(0, None)
        for t in range(group):
            tile = streams[t * HEADS_PER_STEP:(t + 1) * HEADS_PER_STEP]
            outs = [acc_ref[st, 0:V_DIM] / acc_ref[st, V_DIM:V_DIM + 1] for _, _, st in tile]
            o_ref[0, tile[0][0], :] = jnp.concatenate(outs, axis=0).T.astype(BF)


def _attn_call(qt, kc, vct, kl=None, vlt=None):
    b, nh, dq, t = qt.shape
    hkv = kc.shape[1]
    kvps = HEADS_PER_STEP * hkv // nh
    tq = ATTN_TQ
    tstep = min(t, ATTN_TILES_PER_STEP * tq)
    n_tiles = tstep // tq
    group = min(ATTN_TILE_GROUP, n_tiles)
    n_lat = 0 if kl is None else kl.shape[2]
    assert n_lat % 2 == 0, "the skewed key-block schedule handles latent blocks in pairs"
    rows_max = kc.shape[3] if kl is None else max(kc.shape[3], kl.shape[3])
    kvspec = lambda a: pl.BlockSpec((1, kvps) + a.shape[2:], lambda bi, j, i: (bi, j, 0, 0, 0))
    in_specs = [pl.BlockSpec((1, HEADS_PER_STEP, dq, tstep), lambda bi, j, i: (bi, j, 0, i)), kvspec(kc), kvspec(vct)]
    args = [qt, kc, vct]
    if n_lat:
        in_specs += [kvspec(kl), kvspec(vlt)]
        args += [kl, vlt]
    streams = group * HEADS_PER_STEP
    buffer_set = ([pltpu.VMEM((streams, rows_max, tq), F32)] * 2 + [pltpu.VMEM((streams, rows_max, tq), BF)] * 2
                  + [pltpu.VMEM((streams, 1, tq), F32)] * 4)
    return pl.pallas_call(
        functools.partial(_attn_kernel, kvps, n_lat, tq, group),
        out_shape=jax.ShapeDtypeStruct((b, t, nh * V_DIM), BF),
        grid=(b, nh // HEADS_PER_STEP, t // tstep), in_specs=in_specs,
        out_specs=pl.BlockSpec((1, tstep, HEADS_PER_STEP * V_DIM), lambda bi, j, i: (bi, i, j)),
        scratch_shapes=[pltpu.VMEM((n_tiles * HEADS_PER_STEP, 1, tq), F32),
                        pltpu.VMEM((n_tiles * HEADS_PER_STEP, V_ROWS, tq), F32)]
        + buffer_set * min(n_tiles // group, 2),
        compiler_params=_cparams(3), name="attention",
    )(*args)


def _rope_tables(seq, rot_dim):
    quarter = rot_dim // 4
    pos = jnp.arange(seq)
    row = (pos // GRID_W).astype(F32)
    col = (pos % GRID_W).astype(F32)
    inv = ROPE_THETA ** (-jnp.arange(quarter, dtype=F32) / quarter)
    ang_r = inv[:, None] * row[None, :]
    ang_c = inv[:, None] * col[None, :]
    cos = jnp.concatenate([jnp.cos(ang_r)] * 2 + [jnp.cos(ang_c)] * 2, axis=0)
    sin = jnp.concatenate([-jnp.sin(ang_r), jnp.sin(ang_r), -jnp.sin(ang_c), jnp.sin(ang_c)], axis=0)
    return cos, sin


def _col(v):
    return v.reshape(-1, 1)


def _row(v):
    return v.reshape(1, -1)


def kernel(x, c, ctx, c_ctx, norm1_g, norm2_g, w_mod, b_mod, w_ff1, w_ff2, conv_w_pw1, conv_w_dw, conv_b_dw, conv_ln_g, conv_ln_b, conv_w_pw2, gqa_w_qkv, gqa_q_norm, gqa_k_norm, gqa_w_o, pool_w, pool_scale, mla_w_dq, mla_q_lora_norm, mla_w_uq, mla_w_dkv, mla_kv_lora_norm, mla_w_ukv, mla_q_norm, mla_k_norm, mla_w_o):
    batch, seq, d = x.shape
    depth = w_mod.shape[0]

    cvec = jnp.concatenate([c, c_ctx[None], jnp.zeros((8 - batch - 1, d), F32)], axis=0)
    mod = _mod_call(cvec, w_mod, b_mod).reshape(depth, 8, 6, d)
    mod = jnp.pad(mod, ((0, 0), (0, 0), (0, 2), (0, 0)))
    mod_l = mod[:, :batch]
    mod_c = jnp.broadcast_to(mod[:, batch:batch + 1], (depth, batch, 8, d))

    tab_gqa = _rope_tables(seq, GQA_HEAD_DIM)
    tab_mla = _rope_tables(seq, MLA_ROPE)

    def split_kv(k, vt):
        b_, h_, l_, dq_ = k.shape
        n_ = vt.shape[2]
        return k.reshape(b_, h_, n_, l_ // n_, dq_), vt

    w1_all = w_ff1.astype(BF)
    w2_all = w_ff2.astype(BF)
    xl, xc = x, ctx
    for i in range(depth):
        m = i % N_MIXERS
        j = i // N_MIXERS
        with_ctx = i < depth - 1
        g1 = _row(norm1_g[i])
        g2 = _row(norm2_g[i])
        ml, mc = mod_l[i], mod_c[i]
        oscale = None
        if m == 0:
            wp1 = conv_w_pw1[j].astype(BF)
            wo = conv_w_pw2[j].astype(BF)
            cargs = (jnp.pad(conv_w_dw[j], ((0, 1), (0, 0))), _row(conv_b_dw[j]), _row(conv_ln_g[j]), _row(conv_ln_b[j]))
            al = _dwconv_call(_glu_call(xl, ml, g1, wp1), *cargs)
            ac = _dwconv_call(_glu_call(xc, mc, g1, wp1), *cargs) if with_ctx else None
        elif m == 1:
            wt = gqa_w_qkv[j].T.astype(BF)
            wo = gqa_w_o[j].astype(BF)
            qg, kg = _col(gqa_q_norm[j]), _col(gqa_k_norm[j])
            ql, kl, vl = _gqa_proj_call(xl, ml, g1, wt, qg, kg, tab_gqa)
            qc, kc, vc = _gqa_proj_call(xc, mc, g1, wt, qg, kg, None)
            kc5, vc5 = split_kv(kc, vc)
            al = _attn_call(ql, kc5, vc5, *split_kv(kl, vl))
            ac = _attn_call(qc, kc5, vc5) if with_ctx else None
        elif m == 2:
            eye = jnp.eye(len(POOL_WINDOWS), dtype=F32)
            wo = jnp.einsum('gcd,gh->gchd', pool_w[j], eye).reshape(d, d).astype(BF)
            oscale = _row(pool_scale[j])
            al = _pool_call(xl, ml, g1)
            ac = _pool_call(xc, mc, g1) if with_ctx else None
        else:
            wo = mla_w_o[j].astype(BF)
            pargs = (mla_w_dq[j].T.astype(BF), _col(mla_q_lora_norm[j]), mla_w_uq[j].T.astype(BF),
                     mla_w_dkv[j].T.astype(BF), _col(mla_kv_lora_norm[j]), mla_w_ukv[j].T.astype(BF),
                     _col(mla_q_norm[j]), _col(mla_k_norm[j]))
            ql, kl, vl = _mla_proj_call(xl, ml, g1, *pargs, tab_mla, True)
            pc = _mla_proj_call(xc, mc, g1, *pargs, None, with_ctx)
            kc5, vc5 = split_kv(*pc[-2:])
            al = _attn_call(ql, kc5, vc5, *split_kv(kl, vl))
            ac = _attn_call(pc[0], kc5, vc5) if with_ctx else None
        xl = _post_call(xl, al, ml, g2, wo, w1_all, w2_all, i, oscale)
        if with_ctx:
            xc = _post_call(xc, ac, mc, g2, wo, w1_all, w2_all, i, oscale)
    return xl
```
